```python
import jax, jax.numpy as jnp
from jax import lax
import numpy as np

D_MODEL = 2048
BATCH = 1
SEQ = 16384
DEPTH = 2
DEC_BATCH = 8
DEC_SEQ = 16
PAST_LEN = 2048

CHUNK = 64
N_BRANCH = 4
BRANCH_W = D_MODEL // 4
HEAD_DIM = 128
N_HEADS = BRANCH_W // HEAD_DIM
GMLP_CHUNK = 128
CONV_W = 31
N_MEM = 256
Q_BLOCK = 128
EPS = 1e-6
SPLIT_IDX = tuple(BRANCH_W * i for i in (1, 2, 3, 4, 5, 6, 7, 9, 10, 11, 12))
IN_W = 12 * BRANCH_W + N_BRANCH * D_MODEL

kernel_name = "hybrid_streaming_encoder_step"


def rms_norm(x, g):
    x32 = x.astype(jnp.float32)
    y = x32 * lax.rsqrt(jnp.mean(x32 * x32, axis=-1, keepdims=True) + EPS)
    return (y * g.astype(jnp.float32)).astype(x.dtype)


def layer_norm(x, g, b):
    x32 = x.astype(jnp.float32)
    mu = jnp.mean(x32, axis=-1, keepdims=True)
    xc = x32 - mu
    y = xc * lax.rsqrt(jnp.mean(xc * xc, axis=-1, keepdims=True) + EPS)
    return (y * g.astype(jnp.float32) + b.astype(jnp.float32)).astype(x.dtype)


def gmlp_mask():
    i = np.arange(GMLP_CHUNK)
    return jnp.asarray((i[None, :] // CHUNK) <= (i[:, None] // CHUNK))


def spatial_gate(vn, ws, bs, is_prompt):
    ws_m = jnp.where(gmlp_mask()[None], ws, 0.0)
    B, T, _ = vn.shape
    if is_prompt:
        vc = vn.reshape(B, T // GMLP_CHUNK, GMLP_CHUNK, N_HEADS, HEAD_DIM)
        s = jnp.einsum('gij,bnjgc->bnigc', ws_m, vc) + bs.T[None, None, :, :, None]
    else:
        vc = vn.reshape(B, T, N_HEADS, HEAD_DIM)
        s = jnp.einsum('gij,bjgc->bigc', ws_m[:, :T, :T], vc) + bs.T[:T][None, :, :, None]
    return s.reshape(B, T, BRANCH_W)


def stick_breaking(q, k, v, q_pos, k_pos):
    z = jnp.einsum('bqhd,bkhd->bhqk', q, k).astype(jnp.float32) * (HEAD_DIM ** -0.5)
    causal = k_pos[None, :] < q_pos[:, None]
    log_keep = jnp.where(causal, jax.nn.log_sigmoid(-z), 0.0)
    later = lax.cumsum(log_keep, axis=3, reverse=True) - log_keep
    w = jnp.where(causal, jnp.exp(jax.nn.log_sigmoid(z) + later), 0.0)
    return jnp.einsum('bhqk,bkhd->bqhd', w.astype(v.dtype), v)


def stick_breaking_prompt(q, k, v):
    B, S, H, Dh = q.shape
    nb = S // Q_BLOCK
    qb = q.reshape(B, nb, Q_BLOCK, H, Dh).transpose(1, 0, 2, 3, 4)
    pos = jnp.arange(S, dtype=jnp.int32)
    pb = pos.reshape(nb, Q_BLOCK)
    out = lax.map(lambda a: stick_breaking(a[0], k, v, a[1], pos), (qb, pb))
    return out.transpose(1, 0, 2, 3, 4).reshape(B, S, H, Dh)


def causal_depthwise_conv(u, left, w, b):
    xp = jnp.concatenate([left.astype(u.dtype), u], axis=1)
    y = lax.conv_general_dilated(xp, w[:, None, :], window_strides=(1,), padding='VALID',
                                 dimension_numbers=('NWC', 'WIO', 'NWC'),
                                 feature_group_count=BRANCH_W)
    return y + b


def memory_kv(mem, mem_norm_g, w_mem_kv, k_norm_g):
    B, N, _ = mem.shape
    kv = rms_norm(mem, mem_norm_g) @ w_mem_kv
    k, v = jnp.split(kv, 2, axis=-1)
    k = rms_norm(k.reshape(B, N, N_HEADS, HEAD_DIM), k_norm_g)
    return k, v.reshape(B, N, N_HEADS, HEAD_DIM)


def layer(x, norm_g, w_in, gmlp_ln_g, gmlp_ln_b, gmlp_ws, gmlp_bs, conv_w, conv_b, conv_ln_g, conv_ln_b,
          q_norm_g, b_gate, w_branch, w_out, mem_k, mem_v, past_k, past_v, conv_left):
    is_prompt = past_k is None
    B, T, _ = x.shape
    h = rms_norm(x, norm_g)
    (a_u, a_v, a_g, b_q, b_k, b_v, b_g, c_in, c_g, m_q, m_g, gate_logits) = jnp.split(h @ w_in, SPLIT_IDX, axis=-1)

    vn = layer_norm(a_v, gmlp_ln_g, gmlp_ln_b)
    y_a = a_u * spatial_gate(vn, gmlp_ws, gmlp_bs, is_prompt) * jax.nn.silu(a_g)

    q = b_q.reshape(B, T, N_HEADS, HEAD_DIM)
    k = b_k.reshape(B, T, N_HEADS, HEAD_DIM)
    v = b_v.reshape(B, T, N_HEADS, HEAD_DIM)
    if is_prompt:
        o = stick_breaking_prompt(q, k, v)
    else:
        P = past_k.shape[1]
        kk = jnp.concatenate([past_k.astype(k.dtype), k], axis=1)
        vv = jnp.concatenate([past_v.astype(v.dtype), v], axis=1)
        o = stick_breaking(q, kk, vv, P + jnp.arange(T, dtype=jnp.int32), jnp.arange(P + T, dtype=jnp.int32))
    y_b = o.reshape(B, T, BRANCH_W) * jax.nn.silu(b_g)

    c_a, c_b = jnp.split(c_in, 2, axis=-1)
    glu = c_a * jax.nn.sigmoid(c_b)
    cv = causal_depthwise_conv(glu, conv_left, conv_w, conv_b)
    y_c = jax.nn.silu(layer_norm(cv, conv_ln_g, conv_ln_b)) * jax.nn.silu(c_g)
    new_conv = jnp.concatenate([conv_left.astype(glu.dtype), glu], axis=1)[:, -(CONV_W - 1):]

    qm = rms_norm(m_q.reshape(B, T, N_HEADS, HEAD_DIM), q_norm_g)
    s = jnp.einsum('bqhd,bkhd->bhqk', qm, mem_k.astype(qm.dtype)).astype(jnp.float32) * (HEAD_DIM ** -0.5)
    pm = jax.nn.softmax(s, axis=-1)
    om = jnp.einsum('bhqk,bkhd->bqhd', pm.astype(x.dtype), mem_v.astype(x.dtype))
    y_m = om.reshape(B, T, BRANCH_W) * jax.nn.silu(m_g)

    gl = gate_logits.reshape(B, T, N_BRANCH, D_MODEL) + b_gate
    branches = (y_a, y_b, y_c, y_m)
    mixed = jax.nn.sigmoid(gl[:, :, 0]) * (branches[0] @ w_branch[0])
    for n in range(1, N_BRANCH):
        mixed = mixed + jax.nn.sigmoid(gl[:, :, n]) * (branches[n] @ w_branch[n])
    return x + mixed @ w_out, k, v, new_conv, vn


def setup_inputs(seed: int = 0) -> dict:
    key = jax.random.key(seed)
    ks = jax.random.split(key, 26)
    f32 = jnp.float32

    def nrm(k, shape, scale):
        return jax.random.normal(k, shape, f32) * scale

    return {
        "x_prompt": nrm(ks[0], (BATCH, SEQ, D_MODEL), 1.0),
        "x_sample": nrm(ks[1], (DEC_BATCH, DEC_SEQ, D_MODEL), 1.0),
        "mem_prompt": nrm(ks[2], (BATCH, N_MEM, D_MODEL), 1.0),
        "cache_sb_k": nrm(ks[3], (DEPTH, DEC_BATCH, PAST_LEN, N_HEADS, HEAD_DIM), 1.0),
        "cache_sb_v": nrm(ks[4], (DEPTH, DEC_BATCH, PAST_LEN, N_HEADS, HEAD_DIM), 1.0),
        "state_conv": nrm(ks[5], (DEPTH, DEC_BATCH, CONV_W - 1, BRANCH_W), 0.5),
        "cache_mem_k": nrm(ks[6], (DEPTH, DEC_BATCH, N_MEM, N_HEADS, HEAD_DIM), 1.0),
        "cache_mem_v": nrm(ks[7], (DEPTH, DEC_BATCH, N_MEM, N_HEADS, HEAD_DIM), 1.0),
        "norm_g": 1.0 + nrm(ks[8], (DEPTH, D_MODEL), 0.05),
        "w_in": nrm(ks[9], (DEPTH, D_MODEL, IN_W), D_MODEL ** -0.5),
        "gmlp_ln_g": 1.0 + nrm(ks[10], (DEPTH, BRANCH_W), 0.05),
        "gmlp_ln_b": nrm(ks[11], (DEPTH, BRANCH_W), 0.02),
        "gmlp_ws": nrm(ks[12], (DEPTH, N_HEADS, GMLP_CHUNK, GMLP_CHUNK), GMLP_CHUNK ** -0.5),
        "gmlp_bs": 1.0 + nrm(ks[13], (DEPTH, N_HEADS, GMLP_CHUNK), 0.01),
        "conv_w": nrm(ks[14], (DEPTH, CONV_W, BRANCH_W), CONV_W ** -0.5),
        "conv_b": nrm(ks[15], (DEPTH, BRANCH_W), 0.02),
        "conv_ln_g": 1.0 + nrm(ks[16], (DEPTH, BRANCH_W), 0.05),
        "conv_ln_b": nrm(ks[17], (DEPTH, BRANCH_W), 0.02),
        "mem_norm_g": 1.0 + nrm(ks[18], (DEPTH, D_MODEL), 0.05),
        "w_mem_kv": nrm(ks[19], (DEPTH, D_MODEL, 2 * BRANCH_W), D_MODEL ** -0.5),
        "q_norm_g": 1.0 + nrm(ks[20], (DEPTH, HEAD_DIM), 0.05),
        "k_norm_g": 1.0 + nrm(ks[21], (DEPTH, HEAD_DIM), 0.05),
        "b_gate": nrm(ks[22], (DEPTH, N_BRANCH, D_MODEL), 0.02),
        "w_branch": nrm(ks[23], (DEPTH, N_BRANCH, BRANCH_W, D_MODEL), BRANCH_W ** -0.5),
        "w_out": nrm(ks[24], (DEPTH, D_MODEL, D_MODEL), D_MODEL ** -0.5),
    }


def reference(x_prompt, x_sample, mem_prompt, cache_sb_k, cache_sb_v, state_conv, cache_mem_k, cache_mem_v,
              norm_g, w_in, gmlp_ln_g, gmlp_ln_b, gmlp_ws, gmlp_bs, conv_w, conv_b, conv_ln_g, conv_ln_b,
              mem_norm_g, w_mem_kv, q_norm_g, k_norm_g, b_gate, w_branch, w_out):
    xp, xs = x_prompt, x_sample
    kp_l, vp_l, cp_l, mkp_l, mvp_l = [], [], [], [], []
    ks_l, vs_l, cs_l, gs_l = [], [], [], []
    zero_left = jnp.zeros((x_prompt.shape[0], CONV_W - 1, BRANCH_W), x_prompt.dtype)
    for l in range(DEPTH):
        params = (norm_g[l], w_in[l], gmlp_ln_g[l], gmlp_ln_b[l], gmlp_ws[l], gmlp_bs[l], conv_w[l], conv_b[l],
                  conv_ln_g[l], conv_ln_b[l], q_norm_g[l], b_gate[l], w_branch[l], w_out[l])
        mk, mv = memory_kv(mem_prompt, mem_norm_g[l], w_mem_kv[l], k_norm_g[l])
        xp, kp, vp, cp, _ = layer(xp, *params, mk, mv, None, None, zero_left)
        kp_l.append(kp); vp_l.append(vp); cp_l.append(cp); mkp_l.append(mk); mvp_l.append(mv)
        xs, kn, vn_, cn, gn = layer(xs, *params, cache_mem_k[l], cache_mem_v[l],
                                    cache_sb_k[l], cache_sb_v[l], state_conv[l])
        ks_l.append(kn); vs_l.append(vn_); cs_l.append(cn); gs_l.append(gn)
    sb_k_prompt = jnp.stack(kp_l)
    sb_v_prompt = jnp.stack(vp_l)
    conv_prompt = jnp.stack(cp_l)
    mem_k_prompt = jnp.stack(mkp_l)
    mem_v_prompt = jnp.stack(mvp_l)
    sb_k_sample = jnp.stack(ks_l)
    sb_v_sample = jnp.stack(vs_l)
    conv_sample = jnp.stack(cs_l)
    gmlp_v_sample = jnp.stack(gs_l)
    return (xp, xs, sb_k_prompt, sb_v_prompt, conv_prompt, mem_k_prompt, mem_v_prompt,
            sb_k_sample, sb_v_sample, conv_sample, gmlp_v_sample)
```

```python
import functools

import jax
import jax.numpy as jnp
from jax import lax
from jax.experimental import pallas as pl
from jax.experimental.pallas import tpu as pltpu

F32 = jnp.float32
BF16 = jnp.bfloat16

HEAD_DIM = 128
N_HEADS = 4
BRANCH_W = N_HEADS * HEAD_DIM
N_BRANCH = 4
GMLP_CHUNK = 128
CHUNK = 64
CONV_W = 31
HALO = 32
EPS = 1e-6
SCALE = HEAD_DIM ** -0.5
U_AU, U_AV, U_AG, U_BQ, U_BK, U_BV, U_BG, U_CA, U_CB, U_CG, U_MQ, U_MG, U_GATES = (
    0, 1, 2, 3, 4, 5, 6, 7, 8, 9, 10, 11, 12)
SB_EXIT = -110.0
V7X_VMEM_BYTES = 64 * 1024 * 1024


def _vmem_limit(estimate_bytes):
    return int(min(max(estimate_bytes * 5 // 4 + (4 << 20), 16 << 20), V7X_VMEM_BYTES - (6 << 20)))


def _sigmoid(x):
    return 1.0 / (1.0 + jnp.exp(-x))


def _silu(x):
    return x * _sigmoid(x)


def _layer_norm(x, g, b):
    mu = jnp.mean(x, axis=-1, keepdims=True)
    xc = x - mu
    return xc * lax.rsqrt(jnp.mean(xc * xc, axis=-1, keepdims=True) + EPS) * g + b


def _rms_norm(x, g):
    return x * lax.rsqrt(jnp.mean(x * x, axis=-1, keepdims=True) + EPS) * g


def _inproj_kernel(x_ref, g_ref, w_ref, *rest, head_norm, row_chunk):
    if head_norm:
        kg_ref, o_ref, h_scr = rest
    else:
        o_ref, h_scr = rest
    j = pl.program_id(1)
    tm = x_ref.shape[0]

    @pl.when(j == 0)
    def _():
        def body(c, carry):
            r0 = pl.multiple_of(c * row_chunk, row_chunk)
            x = x_ref[pl.ds(r0, row_chunk), :]
            h_scr[pl.ds(r0, row_chunk), :] = _rms_norm(x, g_ref[...]).astype(BF16)
            return carry
        lax.fori_loop(0, tm // row_chunk, body, 0)

    acc = jnp.dot(h_scr[...], w_ref[...], preferred_element_type=F32)
    if head_norm:
        @pl.when(j == 0)
        def _():
            for hh in range(acc.shape[1] // HEAD_DIM):
                a = acc[:, hh * HEAD_DIM:(hh + 1) * HEAD_DIM]
                o_ref[:, hh * HEAD_DIM:(hh + 1) * HEAD_DIM] = _rms_norm(a, kg_ref[...])

        @pl.when(j != 0)
        def _():
            o_ref[...] = acc
    else:
        o_ref[...] = acc


def _inproj(x, g, w_bf, *, tm, tn, kg=None):
    rows, d = x.shape
    n = w_bf.shape[1]
    tm = min(tm, rows)
    assert rows % tm == 0 and n % tn == 0
    head_norm = kg is not None
    row_chunk = min(tm, 32)
    in_specs = [
        pl.BlockSpec((tm, d), lambda i, j: (i, 0)),
        pl.BlockSpec((1, d), lambda i, j: (0, 0)),
        pl.BlockSpec((d, tn), lambda i, j: (0, j)),
    ]
    args = [x, g.reshape(1, d), w_bf]
    if head_norm:
        assert tn == BRANCH_W
        in_specs.append(pl.BlockSpec((1, HEAD_DIM), lambda i, j: (0, 0)))
        args.append(kg.reshape(1, HEAD_DIM))
    est = 2 * tm * d * 4 + tm * d * 2 + 2 * d * tn * 2 + 3 * tm * tn * 4
    return pl.pallas_call(
        functools.partial(_inproj_kernel, head_norm=head_norm, row_chunk=row_chunk),
        grid=(rows // tm, n // tn),
        in_specs=in_specs,
        out_specs=pl.BlockSpec((tm, tn), lambda i, j: (i, j)),
        out_shape=jax.ShapeDtypeStruct((rows, n), F32),
        scratch_shapes=[pltpu.VMEM((tm, d), BF16)],
        compiler_params=pltpu.CompilerParams(
            dimension_semantics=("arbitrary", "arbitrary"), vmem_limit_bytes=_vmem_limit(est)),
        name="inproj_kv" if head_norm else "inproj",
    )(*args)


def _sb_kernel(q_ref, kd_ref, vd_ref, g_ref, kp_ref, vp_ref, o_ref, *, n_past_fn):
    bq = q_ref.shape[0]
    qi = pl.program_id(2)
    q = q_ref[...].astype(BF16)

    jj = lax.broadcasted_iota(jnp.int32, (2 * bq, 2 * bq), 0) & (bq - 1)
    ss = lax.broadcasted_iota(jnp.int32, (2 * bq, 2 * bq), 1)
    tri = jnp.where(ss >= bq, 1.0, jnp.where(jj > ss, 1.0, 0.0)).astype(BF16)
    row = lax.broadcasted_iota(jnp.int32, (bq, bq), 0)
    col = lax.broadcasted_iota(jnp.int32, (bq, bq), 1)
    causal = col < row

    def block(k_blk, v_blk, carry, acc, diag):
        z = lax.dot_general(q, k_blk, (((1,), (1,)), ((), ())), preferred_element_type=F32) * SCALE
        log_keep = -(jnp.maximum(z, 0.0) + jnp.log(1.0 + jnp.exp(-jnp.abs(z))))
        if diag:
            log_keep = jnp.where(causal, log_keep, 0.0)
        hi = log_keep.astype(BF16)
        lo = (log_keep - hi.astype(F32)).astype(BF16)
        cum = jnp.dot(jnp.concatenate([hi, lo], axis=1), tri, preferred_element_type=F32)
        w = jnp.exp(z + log_keep + carry + cum[:, :bq])
        if diag:
            w = jnp.where(causal, w, 0.0)
        acc = acc + jnp.dot(w.astype(BF16), v_blk, preferred_element_type=F32)
        return carry + cum[:, bq:], acc

    zeros = jnp.zeros((bq, HEAD_DIM), F32)
    carry, acc = block(kd_ref[...].astype(BF16), vd_ref[...].astype(BF16), zeros, zeros, True)

    def cond(state):
        kb, _, _, top = state
        return jnp.logical_and(kb >= 0, top > SB_EXIT)

    def body(state):
        kb, carry, acc, _ = state
        off = pl.multiple_of(kb * bq, bq)
        k_blk = kp_ref[pl.ds(off, bq), :].astype(BF16)
        v_blk = vp_ref[pl.ds(off, bq), :].astype(BF16)
        carry, acc = block(k_blk, v_blk, carry, acc, False)
        return kb - 1, carry, acc, jnp.max(carry)

    last_past = jnp.asarray(n_past_fn(qi) - 1, jnp.int32)
    _, _, acc, _ = lax.while_loop(cond, body, (last_past, carry, acc, jnp.max(carry)))
    o_ref[...] = (acc * _silu(g_ref[...])).astype(o_ref.dtype)


def _sb_attn(qkvg, q_u, k_u, v_u, g_u, past_k, past_v, pk_u, pv_u, pk_row0, *, n_batch, n_qblk, past_rows,
             causal_past):
    bq = HEAD_DIM
    n_past_fn = (lambda qi: qi) if causal_past else (lambda qi: past_rows // bq)
    blk = lambda u: pl.BlockSpec((bq, HEAD_DIM), lambda b, h, i, u=u: (b * n_qblk + i, u + h))
    past = lambda u: pl.BlockSpec((past_rows, HEAD_DIM), lambda b, h, i, u=u: (pk_row0 + b, u + h))
    est = 2 * 2 * past_rows * HEAD_DIM * 4 + (2 << 20)
    return pl.pallas_call(
        functools.partial(_sb_kernel, n_past_fn=n_past_fn),
        grid=(n_batch, N_HEADS, n_qblk),
        in_specs=[blk(q_u), blk(k_u), blk(v_u), blk(g_u), past(pk_u), past(pv_u)],
        out_specs=pl.BlockSpec((bq, HEAD_DIM), lambda b, h, i: (b * n_qblk + i, h)),
        out_shape=jax.ShapeDtypeStruct((n_batch * n_qblk * bq, BRANCH_W), BF16),
        compiler_params=pltpu.CompilerParams(
            dimension_semantics=("arbitrary", "arbitrary", "arbitrary"), vmem_limit_bytes=_vmem_limit(est)),
        name="sb_attn",
    )(qkvg, qkvg, qkvg, qkvg, past_k, past_v)


def _branch_kernel(au_ref, av_ref, ag_ref, ca_ref, cb_ref, cg_ref, mq_ref, mg_ref, left_ref, mk_ref, mv_ref,
                   lng_ref, lnb_ref, wm_ref, pb_ref, cw_ref, cbias_ref, clng_ref, clnb_ref, qng_ref,
                   ya_ref, yc_ref, ym_ref, vn_ref, nconv_ref, vn_scr, xp_scr, *, nb, tb, conv_chunk):
    rows = nb * tb

    vn = _layer_norm(av_ref[...], lng_ref[...], lnb_ref[...])
    vn_ref[...] = vn
    vn_scr[...] = vn.astype(BF16)
    for c in range(rows // GMLP_CHUNK):
        rs = slice(c * GMLP_CHUNK, (c + 1) * GMLP_CHUNK)
        for g in range(N_HEADS):
            cs = slice(g * HEAD_DIM, (g + 1) * HEAD_DIM)
            s = jnp.dot(wm_ref[g], vn_scr[rs, cs], preferred_element_type=F32) + pb_ref[g]
            ya_ref[rs, cs] = (au_ref[rs, cs] * s * _silu(ag_ref[rs, cs])).astype(ya_ref.dtype)

    first = pl.program_id(1) == 0
    for b in range(nb):
        seg = slice(b * tb, (b + 1) * tb)

        @pl.when(first)
        def _():
            xp_scr[b, 0:HALO, :] = left_ref[b]

        xp_scr[b, HALO:HALO + tb, :] = ca_ref[seg, :] * _sigmoid(cb_ref[seg, :])
        for c in range(tb // conv_chunk):
            r0 = c * conv_chunk
            acc = jnp.broadcast_to(cbias_ref[...], (conv_chunk, BRANCH_W))
            for w in range(CONV_W):
                start = HALO - (CONV_W - 1) + r0 + w
                acc = acc + xp_scr[b, start:start + conv_chunk, :] * cw_ref[w:w + 1, :]
            cn = _layer_norm(acc, clng_ref[...], clnb_ref[...])
            rs = slice(b * tb + r0, b * tb + r0 + conv_chunk)
            yc_ref[rs, :] = (_silu(cn) * _silu(cg_ref[rs, :])).astype(yc_ref.dtype)
        tail = xp_scr[b, tb:tb + HALO, :]
        nconv_ref[b] = tail
        xp_scr[b, 0:HALO, :] = tail

    for b in range(nb):
        seg = slice(b * tb, (b + 1) * tb)
        for h in range(N_HEADS):
            cs = slice(h * HEAD_DIM, (h + 1) * HEAD_DIM)
            qn = _rms_norm(mq_ref[seg, cs], qng_ref[...]).astype(BF16)
            s = lax.dot_general(qn, mk_ref[b, :, cs].astype(BF16), (((1,), (1,)), ((), ())),
                                preferred_element_type=F32) * SCALE
            e = jnp.exp(s - jnp.max(s, axis=-1, keepdims=True))
            p = e / jnp.sum(e, axis=-1, keepdims=True)
            o = jnp.dot(p.astype(BF16), mv_ref[b, :, cs].astype(BF16), preferred_element_type=F32)
            ym_ref[seg, cs] = (o * _silu(mg_ref[seg, cs])).astype(ym_ref.dtype)


def _branches(proj, left32, mk_arr, mv_arr, mk_map, mv_map, p, *, nb, tb, n_seg_tiles, n_row_tiles):
    rows = nb * tb
    total = proj.shape[0]
    assert total == n_seg_tiles * n_row_tiles * rows and (nb == 1 or n_row_tiles == 1)
    assert rows % GMLP_CHUNK == 0
    n_mem = mk_arr.shape[1]
    conv_chunk = min(tb, 32)
    col = lambda u: pl.BlockSpec((rows, BRANCH_W), lambda s, i, u=u: (s * n_row_tiles + i, u))
    const = lambda shape: pl.BlockSpec(shape, lambda s, i: (0,) * len(shape))
    in_specs = [col(U_AU), col(U_AV), col(U_AG), col(U_CA), col(U_CB), col(U_CG), col(U_MQ), col(U_MG),
                pl.BlockSpec((nb, HALO, BRANCH_W), lambda s, i: (s, 0, 0)),
                pl.BlockSpec((nb, n_mem, BRANCH_W), mk_map),
                pl.BlockSpec((nb, n_mem, BRANCH_W), mv_map),
                const((1, BRANCH_W)), const((1, BRANCH_W)),
                const((N_HEADS, GMLP_CHUNK, GMLP_CHUNK)), const((N_HEADS, GMLP_CHUNK, HEAD_DIM)),
                const((CONV_W, BRANCH_W)), const((1, BRANCH_W)), const((1, BRANCH_W)), const((1, BRANCH_W)),
                const((1, HEAD_DIM))]
    row_out = pl.BlockSpec((rows, BRANCH_W), lambda s, i: (s * n_row_tiles + i, 0))
    out_specs = [row_out, row_out, row_out, row_out,
                 pl.BlockSpec((nb, HALO, BRANCH_W), lambda s, i: (s, 0, 0))]
    out_shape = [jax.ShapeDtypeStruct((total, BRANCH_W), BF16)] * 3 + [
        jax.ShapeDtypeStruct((total, BRANCH_W), F32),
        jax.ShapeDtypeStruct((n_seg_tiles * nb, HALO, BRANCH_W), F32)]
    est = 2 * (8 * rows * BRANCH_W * 4 + 2 * nb * n_mem * BRANCH_W * 4 + 4 * rows * BRANCH_W * 4) + (4 << 20)
    return pl.pallas_call(
        functools.partial(_branch_kernel, nb=nb, tb=tb, conv_chunk=conv_chunk),
        grid=(n_seg_tiles, n_row_tiles),
        in_specs=in_specs,
        out_specs=out_specs,
        out_shape=out_shape,
        scratch_shapes=[pltpu.VMEM((rows, BRANCH_W), BF16), pltpu.VMEM((nb, HALO + tb, BRANCH_W), F32)],
        compiler_params=pltpu.CompilerParams(
            dimension_semantics=("arbitrary", "arbitrary"), vmem_limit_bytes=_vmem_limit(est)),
        name="branches",
    )(proj, proj, proj, proj, proj, proj, proj, proj, left32, mk_arr, mv_arr,
      p["gmlp_ln_g"], p["gmlp_ln_b"], p["mix_w"], p["mix_b"], p["conv_w"], p["conv_b"], p["conv_ln_g"],
      p["conv_ln_b"], p["q_norm_g"])


def _merge_kernel(ya_ref, yb_ref, yc_ref, ym_ref, gl0_ref, gl1_ref, gl2_ref, gl3_ref, bg_ref, wb_ref, wo_ref, x_ref,
                  o_ref, mix_scr, *, col_chunk):
    ys = (ya_ref, yb_ref, yc_ref, ym_ref)
    gls = (gl0_ref, gl1_ref, gl2_ref, gl3_ref)
    d = x_ref.shape[1]
    for c in range(d // col_chunk):
        cs = slice(c * col_chunk, (c + 1) * col_chunk)
        mixed = None
        for n in range(N_BRANCH):
            t = jnp.dot(ys[n][...], wb_ref[n, :, cs], preferred_element_type=F32)
            t = _sigmoid(gls[n][:, cs] + bg_ref[n:n + 1, cs]) * t
            mixed = t if mixed is None else mixed + t
        mix_scr[:, cs] = mixed.astype(BF16)
    o_ref[...] = x_ref[...] + jnp.dot(mix_scr[...], wo_ref[...], preferred_element_type=F32)


def _merge(ya, yb, yc, ym, proj, b_gate, wb_bf, wo_bf, x, *, tm):
    rows, d = x.shape
    tm = min(tm, rows)
    assert rows % tm == 0 and (U_GATES * BRANCH_W) % d == 0
    gate0 = U_GATES * BRANCH_W // d
    ysp = pl.BlockSpec((tm, BRANCH_W), lambda i: (i, 0))
    gl = lambda n: pl.BlockSpec((tm, d), lambda i, n=n: (i, gate0 + n))
    resident = lambda shape: pl.BlockSpec(shape, lambda i: (0,) * len(shape), pipeline_mode=pl.Buffered(1))
    est = (N_BRANCH * BRANCH_W * d * 2 + d * d * 2 + 2 * (4 * tm * BRANCH_W * 2 + 4 * tm * d * 4 + 2 * tm * d * 4)
           + tm * d * 2 + 4 * tm * d * 4)
    return pl.pallas_call(
        functools.partial(_merge_kernel, col_chunk=BRANCH_W),
        grid=(rows // tm,),
        in_specs=[ysp, ysp, ysp, ysp, gl(0), gl(1), gl(2), gl(3),
                  resident((N_BRANCH, d)), resident((N_BRANCH, BRANCH_W, d)), resident((d, d)),
                  pl.BlockSpec((tm, d), lambda i: (i, 0))],
        out_specs=pl.BlockSpec((tm, d), lambda i: (i, 0)),
        out_shape=jax.ShapeDtypeStruct((rows, d), F32),
        scratch_shapes=[pltpu.VMEM((tm, d), BF16)],
        compiler_params=pltpu.CompilerParams(
            dimension_semantics=("arbitrary",), vmem_limit_bytes=_vmem_limit(est)),
        name="merge",
    )(ya, yb, yc, ym, proj, proj, proj, proj, b_gate, wb_bf, wo_bf, x)


def _gmlp_mask():
    i = jnp.arange(GMLP_CHUNK)
    return (i[None, :] // CHUNK) <= (i[:, None] // CHUNK)


def kernel(x_prompt, x_sample, mem_prompt, cache_sb_k, cache_sb_v, state_conv, cache_mem_k, cache_mem_v, norm_g, w_in, gmlp_ln_g, gmlp_ln_b, gmlp_ws, gmlp_bs, conv_w, conv_b, conv_ln_g, conv_ln_b, mem_norm_g, w_mem_kv, q_norm_g, k_norm_g, b_gate, w_branch, w_out):
    n_b, seq, d = x_prompt.shape
    dec_b, dec_t, _ = x_sample.shape
    depth = w_in.shape[0]
    n_mem = mem_prompt.shape[1]
    past_len = cache_sb_k.shape[2]
    assert seq % 256 == 0 and past_len % HEAD_DIM == 0 and dec_b * dec_t == GMLP_CHUNK and dec_t <= HALO
    assert GMLP_CHUNK % dec_t == 0 and d == N_BRANCH * BRANCH_W

    w_in_bf = w_in.astype(BF16)
    w_mem_bf = w_mem_kv.astype(BF16)
    wb_bf = w_branch.astype(BF16)
    wo_bf = w_out.astype(BF16)

    ws_m = jnp.where(_gmlp_mask()[None, None], gmlp_ws, 0.0)
    mix_w_p = ws_m.astype(BF16)
    mix_b_p = jnp.broadcast_to(gmlp_bs[..., None], gmlp_bs.shape + (HEAD_DIM,))
    eye = jnp.eye(dec_b, dtype=F32)
    corner = ws_m[:, :, :dec_t, :dec_t]
    mix_w_s = jnp.einsum("bc,lgij->lgbicj", eye, corner).reshape(depth, N_HEADS, GMLP_CHUNK, GMLP_CHUNK).astype(BF16)
    mix_b_s = jnp.broadcast_to(jnp.tile(gmlp_bs[:, :, :dec_t], (1, 1, dec_b))[..., None],
                               gmlp_bs.shape + (HEAD_DIM,))

    cache_k2 = cache_sb_k.reshape(depth * dec_b * past_len, BRANCH_W)
    cache_v2 = cache_sb_v.reshape(depth * dec_b * past_len, BRANCH_W)
    cache_mk = cache_mem_k.reshape(depth * dec_b, n_mem, BRANCH_W)
    cache_mv = cache_mem_v.reshape(depth * dec_b, n_mem, BRANCH_W)
    state32 = jnp.pad(state_conv, ((0, 0), (0, 0), (HALO - (CONV_W - 1), 0), (0, 0)))
    zero32 = jnp.zeros((n_b, HALO, BRANCH_W), F32)

    xp = x_prompt.reshape(n_b * seq, d)
    xs = x_sample.reshape(dec_b * dec_t, d)
    tb_p = 256
    outs = {k: [] for k in ("kp", "vp", "cp", "mkp", "mvp", "ks", "vs", "cs", "gs")}
    for l in range(depth):
        common = dict(gmlp_ln_g=gmlp_ln_g[l][None], gmlp_ln_b=gmlp_ln_b[l][None], conv_w=conv_w[l],
                      conv_b=conv_b[l][None], conv_ln_g=conv_ln_g[l][None], conv_ln_b=conv_ln_b[l][None],
                      q_norm_g=q_norm_g[l][None])

        mem_kv = _inproj(mem_prompt.reshape(n_b * n_mem, d), mem_norm_g[l], w_mem_bf[l], tm=256, tn=BRANCH_W,
                         kg=k_norm_g[l]).reshape(n_b, n_mem, 2 * BRANCH_W)
        proj = _inproj(xp, norm_g[l], w_in_bf[l], tm=1024, tn=1024)
        u = BRANCH_W // HEAD_DIM
        yb = _sb_attn(proj, U_BQ * u, U_BK * u, U_BV * u, U_BG * u, proj, proj, U_BK * u, U_BV * u, 0,
                      n_batch=n_b, n_qblk=seq // HEAD_DIM, past_rows=seq, causal_past=True)
        ya, yc, ym, _, nconv = _branches(
            proj, zero32, mem_kv, mem_kv, lambda s, i: (s, 0, 0), lambda s, i: (s, 0, 1),
            dict(common, mix_w=mix_w_p[l], mix_b=mix_b_p[l]),
            nb=1, tb=tb_p, n_seg_tiles=n_b, n_row_tiles=seq // tb_p)
        xp = _merge(ya, yb, yc, ym, proj, b_gate[l], wb_bf[l], wo_bf[l], xp, tm=256)
        outs["kp"].append(proj[:, U_BK * BRANCH_W:(U_BK + 1) * BRANCH_W].reshape(n_b, seq, N_HEADS, HEAD_DIM))
        outs["vp"].append(proj[:, U_BV * BRANCH_W:(U_BV + 1) * BRANCH_W].reshape(n_b, seq, N_HEADS, HEAD_DIM))
        outs["cp"].append(nconv[:, HALO - (CONV_W - 1):])
        outs["mkp"].append(mem_kv[:, :, :BRANCH_W].reshape(n_b, n_mem, N_HEADS, HEAD_DIM))
        outs["mvp"].append(mem_kv[:, :, BRANCH_W:].reshape(n_b, n_mem, N_HEADS, HEAD_DIM))

        proj_s = _inproj(xs, norm_g[l], w_in_bf[l], tm=GMLP_CHUNK, tn=1024)
        qkvg = proj_s[:, U_BQ * BRANCH_W:(U_BG + 1) * BRANCH_W].reshape(dec_b, dec_t, 4 * BRANCH_W)
        qkvg = jnp.pad(qkvg, ((0, 0), (0, HEAD_DIM - dec_t), (0, 0))).reshape(dec_b * HEAD_DIM, 4 * BRANCH_W)
        yb_s = _sb_attn(qkvg, 0, u, 2 * u, 3 * u, cache_k2, cache_v2, 0, 0, l * dec_b,
                        n_batch=dec_b, n_qblk=1, past_rows=past_len, causal_past=False)
        yb_s = yb_s.reshape(dec_b, HEAD_DIM, BRANCH_W)[:, :dec_t].reshape(dec_b * dec_t, BRANCH_W)
        ya_s, yc_s, ym_s, vn_s, nconv_s = _branches(
            proj_s, state32[l], cache_mk, cache_mv, lambda s, i, l=l: (l, 0, 0), lambda s, i, l=l: (l, 0, 0),
            dict(common, mix_w=mix_w_s[l], mix_b=mix_b_s[l]),
            nb=dec_b, tb=dec_t, n_seg_tiles=1, n_row_tiles=1)
        xs = _merge(ya_s, yb_s, yc_s, ym_s, proj_s, b_gate[l], wb_bf[l], wo_bf[l], xs, tm=GMLP_CHUNK)
        outs["ks"].append(proj_s[:, U_BK * BRANCH_W:(U_BK + 1) * BRANCH_W].reshape(dec_b, dec_t, N_HEADS, HEAD_DIM))
        outs["vs"].append(proj_s[:, U_BV * BRANCH_W:(U_BV + 1) * BRANCH_W].reshape(dec_b, dec_t, N_HEADS, HEAD_DIM))
        outs["cs"].append(nconv_s[:, HALO - (CONV_W - 1):])
        outs["gs"].append(vn_s.reshape(dec_b, dec_t, BRANCH_W))

    st = lambda k: jnp.stack(outs[k])
    return (xp.reshape(n_b, seq, d), xs.reshape(dec_b, dec_t, d), st("kp"), st("vp"), st("cp"), st("mkp"), st("mvp"),
            st("ks"), st("vs"), st("cs"), st("gs"))
```

```python
import functools

import jax
import jax.numpy as jnp
from jax import lax
from jax.experimental import pallas as pl
from jax.experimental.pallas import tpu as pltpu

F32 = jnp.float32
BF16 = jnp.bfloat16

HEAD_DIM = 128
N_HEADS = 4
BRANCH_W = N_HEADS * HEAD_DIM
N_BRANCH = 4
GMLP_CHUNK = 128
CHUNK = 64
CONV_W = 31
HALO = 32
EPS = 1e-6
SCALE = HEAD_DIM ** -0.5
U_AU, U_AV, U_AG, U_BQ, U_BK, U_BV, U_BG, U_CA, U_CB, U_CG, U_MQ, U_MG, U_GATES = (
    0, 1, 2, 3, 4, 5, 6, 7, 8, 9, 10, 11, 12)
SB_EXIT = -110.0
SB_DONE = -1e30
SB_STREAMS_SHARED = 4
SB_STREAMS_OWN = 2
V7X_VMEM_BYTES = 64 * 1024 * 1024


def _vmem_limit(estimate_bytes):
    return int(min(max(estimate_bytes * 5 // 4 + (4 << 20), 16 << 20), V7X_VMEM_BYTES - (6 << 20)))


def _sigmoid(x):
    return 1.0 / (1.0 + jnp.exp(-x))


def _silu(x):
    return x * _sigmoid(x)


def _layer_norm(x, g, b):
    mu = jnp.mean(x, axis=-1, keepdims=True)
    xc = x - mu
    return xc * lax.rsqrt(jnp.mean(xc * xc, axis=-1, keepdims=True) + EPS) * g + b


def _rms_norm(x, g):
    return x * lax.rsqrt(jnp.mean(x * x, axis=-1, keepdims=True) + EPS) * g


def _norm_rows(x_ref, g_ref, h_scr, row_chunk):
    def body(c, carry):
        r0 = pl.multiple_of(c * row_chunk, row_chunk)
        h_scr[pl.ds(r0, row_chunk), :] = _rms_norm(x_ref[pl.ds(r0, row_chunk), :], g_ref[...]).astype(BF16)
        return carry
    lax.fori_loop(0, x_ref.shape[0] // row_chunk, body, 0)


def _inproj_kernel(x_ref, g_ref, w_ref, o_ref, kf_ref, vf_ref, kb_ref, vb_ref, h_scr, *, kv_tile, row_chunk):
    j = pl.program_id(1)

    @pl.when(j == 0)
    def _():
        _norm_rows(x_ref, g_ref, h_scr, row_chunk)

    acc = jnp.dot(h_scr[...], w_ref[...], preferred_element_type=F32)
    o_ref[...] = acc

    @pl.when(j == kv_tile)
    def _():
        k, v = acc[:, :BRANCH_W], acc[:, BRANCH_W:]
        kf_ref[...] = k
        vf_ref[...] = v
        kb_ref[...] = k.astype(BF16)
        vb_ref[...] = v.astype(BF16)


def _inproj(x, g, w_bf, *, tm, tn):
    rows, d = x.shape
    n = w_bf.shape[1]
    tm = min(tm, rows)
    assert rows % tm == 0 and n % tn == 0 and tn == 2 * BRANCH_W and (U_BK * BRANCH_W) % tn == 0 and U_BV == U_BK + 1
    kv_tile = U_BK * BRANCH_W // tn
    side = pl.BlockSpec((tm, BRANCH_W), lambda i, j: (i, 0))
    est = 2 * tm * d * 4 + tm * d * 2 + 2 * d * tn * 2 + 3 * tm * tn * 4 + 2 * tm * BRANCH_W * 12
    return pl.pallas_call(
        functools.partial(_inproj_kernel, kv_tile=kv_tile, row_chunk=min(tm, 32)),
        grid=(rows // tm, n // tn),
        in_specs=[pl.BlockSpec((tm, d), lambda i, j: (i, 0)),
                  pl.BlockSpec((1, d), lambda i, j: (0, 0)),
                  pl.BlockSpec((d, tn), lambda i, j: (0, j))],
        out_specs=[pl.BlockSpec((tm, tn), lambda i, j: (i, j)), side, side, side, side],
        out_shape=[jax.ShapeDtypeStruct((rows, n), F32),
                   jax.ShapeDtypeStruct((rows, BRANCH_W), F32), jax.ShapeDtypeStruct((rows, BRANCH_W), F32),
                   jax.ShapeDtypeStruct((rows, BRANCH_W), BF16), jax.ShapeDtypeStruct((rows, BRANCH_W), BF16)],
        scratch_shapes=[pltpu.VMEM((tm, d), BF16)],
        compiler_params=pltpu.CompilerParams(
            dimension_semantics=("arbitrary", "arbitrary"), vmem_limit_bytes=_vmem_limit(est)),
        name="inproj",
    )(x, g.reshape(1, d), w_bf)


def _memkv_kernel(x_ref, g_ref, w_ref, kg_ref, mk_ref, mv_ref, h_scr, *, row_chunk):
    j = pl.program_id(0)

    @pl.when(j == 0)
    def _():
        _norm_rows(x_ref, g_ref, h_scr, row_chunk)

    acc = jnp.dot(h_scr[...], w_ref[...], preferred_element_type=F32)

    @pl.when(j == 0)
    def _():
        for hh in range(N_HEADS):
            cs = slice(hh * HEAD_DIM, (hh + 1) * HEAD_DIM)
            mk_ref[:, cs] = _rms_norm(acc[:, cs], kg_ref[...])

    @pl.when(j == 1)
    def _():
        mv_ref[...] = acc


def _memkv(mem, g, w_bf, kg):
    rows, d = mem.shape
    assert w_bf.shape[1] == 2 * BRANCH_W
    whole = lambda shape: pl.BlockSpec(shape, lambda j: (0,) * len(shape))
    est = 2 * rows * d * 4 + rows * d * 2 + 2 * d * BRANCH_W * 2 + 6 * rows * BRANCH_W * 4
    return pl.pallas_call(
        functools.partial(_memkv_kernel, row_chunk=min(rows, 32)),
        grid=(2,),
        in_specs=[whole((rows, d)), whole((1, d)), pl.BlockSpec((d, BRANCH_W), lambda j: (0, j)),
                  whole((1, HEAD_DIM))],
        out_specs=[whole((rows, BRANCH_W)), whole((rows, BRANCH_W))],
        out_shape=[jax.ShapeDtypeStruct((rows, BRANCH_W), F32)] * 2,
        scratch_shapes=[pltpu.VMEM((rows, d), BF16)],
        compiler_params=pltpu.CompilerParams(
            dimension_semantics=("arbitrary",), vmem_limit_bytes=_vmem_limit(est)),
        name="memkv",
    )(mem, g.reshape(1, d), w_bf, kg.reshape(1, HEAD_DIM))


def _sb_kernel(q_ref, g_ref, kd_ref, vd_ref, kp_ref, vp_ref, o_ref, *, n_streams, shared_past, past_rows):
    bq = HEAD_DIM
    step = pl.program_id(1)
    heads = [slice(h * HEAD_DIM, (h + 1) * HEAD_DIM) for h in range(N_HEADS)]
    chains = [(s, h) for s in range(n_streams) for h in range(N_HEADS)]
    rows_of = lambda s: slice(s * bq, (s + 1) * bq)
    q = [q_ref[rows_of(s), heads[h]].astype(BF16) for s, h in chains]
    if shared_past:
        n_past = [step * n_streams + s for s in range(n_streams)]
        past_base = [0] * n_streams
    else:
        n_past = [past_rows // bq] * n_streams
        past_base = [s * past_rows for s in range(n_streams)]

    jj = lax.broadcasted_iota(jnp.int32, (2 * bq, 2 * bq), 0) & (bq - 1)
    ss = lax.broadcasted_iota(jnp.int32, (2 * bq, 2 * bq), 1)
    tri = jnp.where(ss >= bq, 1.0, jnp.where(jj > ss, 1.0, 0.0)).astype(BF16)
    row = lax.broadcasted_iota(jnp.int32, (bq, bq), 0)
    col = lax.broadcasted_iota(jnp.int32, (bq, bq), 1)
    causal = col < row

    def block(k_blks, v_blks, carry, acc, diag):
        zs, splits = [], []
        for c, (s, h) in enumerate(chains):
            z = lax.dot_general(q[c], k_blks[s][:, heads[h]].astype(BF16), (((1,), (1,)), ((), ())),
                                preferred_element_type=F32) * SCALE
            log_keep = -(jnp.maximum(z, 0.0) + jnp.log(1.0 + jnp.exp(-jnp.abs(z))))
            if diag:
                log_keep = jnp.where(causal, log_keep, 0.0)
            hi = log_keep.astype(BF16)
            lo = (log_keep - hi.astype(F32)).astype(BF16)
            zs.append(z + log_keep)
            splits.append(jnp.concatenate([hi, lo], axis=1))
        cum = jnp.dot(jnp.concatenate(splits, axis=0), tri, preferred_element_type=F32)
        new_carry, new_acc = [], []
        for c, (s, h) in enumerate(chains):
            cum_c = cum[c * bq:(c + 1) * bq]
            w = jnp.exp(zs[c] + carry[c] + cum_c[:, :bq])
            if diag:
                w = jnp.where(causal, w, 0.0)
            new_acc.append(acc[c] + jnp.dot(w.astype(BF16), v_blks[s][:, heads[h]].astype(BF16),
                                            preferred_element_type=F32))
            new_carry.append(carry[c] + cum_c[:, bq:])
        return new_carry, new_acc

    def retire(carry, n_left):
        return tuple(jnp.where(n_left[s] > 0, carry[c], SB_DONE) for c, (s, h) in enumerate(chains))

    def top_of(carry):
        top = carry[0]
        for c in carry[1:]:
            top = jnp.maximum(top, c)
        return jnp.max(top)

    zeros = [jnp.zeros((bq, HEAD_DIM), F32)] * len(chains)
    carry, acc = block([kd_ref[rows_of(s), :] for s in range(n_streams)],
                       [vd_ref[rows_of(s), :] for s in range(n_streams)], zeros, zeros, True)
    carry = retire(carry, n_past)
    longest = n_past[0]
    for n in n_past[1:]:
        longest = jnp.maximum(longest, n)

    def cond(state):
        t, _, _, top = state
        return jnp.logical_and(t < longest, top > SB_EXIT)

    def body(state):
        t, carry, acc, _ = state
        k_blks, v_blks = [], []
        for s in range(n_streams):
            kb = jnp.maximum(n_past[s] - 1 - t, 0)
            off = pl.multiple_of(past_base[s] + kb * bq, bq)
            k_blks.append(kp_ref[pl.ds(off, bq), :])
            v_blks.append(vp_ref[pl.ds(off, bq), :])
        carry, acc = block(k_blks, v_blks, carry, acc, False)
        carry = retire(carry, [n - 1 - t for n in n_past])
        return t + 1, carry, tuple(acc), top_of(carry)

    _, _, acc, _ = lax.while_loop(cond, body, (jnp.int32(0), carry, tuple(acc), top_of(carry)))
    for c, (s, h) in enumerate(chains):
        o_ref[rows_of(s), heads[h]] = (acc[c] * _silu(g_ref[rows_of(s), heads[h]])).astype(o_ref.dtype)


def _sb_attn(q_arr, q_u, g_arr, g_u, kd_arr, kd_u, vd_arr, vd_u, past_k, past_v, past_blk0, *, n_groups, n_steps,
             n_streams, past_rows, shared_past):
    rows = n_streams * HEAD_DIM
    blk = lambda u: pl.BlockSpec((rows, BRANCH_W), lambda g, i, u=u: (g * n_steps + i, u))
    if shared_past:
        past = pl.BlockSpec((past_rows, BRANCH_W), lambda g, i: (past_blk0 + g, 0), pipeline_mode=pl.Buffered(1))
        past_bytes = 2 * past_rows * BRANCH_W * past_k.dtype.itemsize
    else:
        assert n_steps == 1 and past_blk0 % n_streams == 0
        past = pl.BlockSpec((n_streams * past_rows, BRANCH_W), lambda g, i: (past_blk0 // n_streams + g, 0))
        past_bytes = 4 * n_streams * past_rows * BRANCH_W * past_k.dtype.itemsize
    est = past_bytes + 16 * rows * BRANCH_W * 4 + (4 << 20)
    return pl.pallas_call(
        functools.partial(_sb_kernel, n_streams=n_streams, shared_past=shared_past, past_rows=past_rows),
        grid=(n_groups, n_steps),
        in_specs=[blk(q_u), blk(g_u), blk(kd_u), blk(vd_u), past, past],
        out_specs=pl.BlockSpec((rows, BRANCH_W), lambda g, i: (g * n_steps + i, 0)),
        out_shape=jax.ShapeDtypeStruct((n_groups * n_steps * rows, BRANCH_W), BF16),
        compiler_params=pltpu.CompilerParams(
            dimension_semantics=("arbitrary", "arbitrary"), vmem_limit_bytes=_vmem_limit(est)),
        name="sb_attn",
    )(q_arr, g_arr, kd_arr, vd_arr, past_k, past_v)


def _branch_kernel(au_ref, av_ref, ag_ref, ca_ref, cb_ref, cg_ref, mq_ref, mg_ref, left_ref, mk_ref, mv_ref,
                   lng_ref, lnb_ref, wm_ref, pb_ref, cw_ref, cbias_ref, clng_ref, clnb_ref, qng_ref,
                   ya_ref, yc_ref, ym_ref, vn_ref, nconv_ref, vn_scr, xp_scr, *, nb, tb, conv_chunk):
    rows = nb * tb

    vn = _layer_norm(av_ref[...], lng_ref[...], lnb_ref[...])
    vn_ref[...] = vn
    vn_scr[...] = vn.astype(BF16)
    for c in range(rows // GMLP_CHUNK):
        rs = slice(c * GMLP_CHUNK, (c + 1) * GMLP_CHUNK)
        for g in range(N_HEADS):
            cs = slice(g * HEAD_DIM, (g + 1) * HEAD_DIM)
            s = jnp.dot(wm_ref[g], vn_scr[rs, cs], preferred_element_type=F32) + pb_ref[g]
            ya_ref[rs, cs] = (au_ref[rs, cs] * s * _silu(ag_ref[rs, cs])).astype(ya_ref.dtype)

    first = pl.program_id(1) == 0
    for b in range(nb):
        seg = slice(b * tb, (b + 1) * tb)

        @pl.when(first)
        def _():
            xp_scr[b, 0:HALO, :] = left_ref[b]

        xp_scr[b, HALO:HALO + tb, :] = ca_ref[seg, :] * _sigmoid(cb_ref[seg, :])
        for c in range(tb // conv_chunk):
            r0 = c * conv_chunk
            acc = jnp.broadcast_to(cbias_ref[...], (conv_chunk, BRANCH_W))
            for w in range(CONV_W):
                start = HALO - (CONV_W - 1) + r0 + w
                acc = acc + xp_scr[b, start:start + conv_chunk, :] * cw_ref[w:w + 1, :]
            cn = _layer_norm(acc, clng_ref[...], clnb_ref[...])
            rs = slice(b * tb + r0, b * tb + r0 + conv_chunk)
            yc_ref[rs, :] = (_silu(cn) * _silu(cg_ref[rs, :])).astype(yc_ref.dtype)
        tail = xp_scr[b, tb:tb + HALO, :]
        nconv_ref[b] = tail
        xp_scr[b, 0:HALO, :] = tail

    for b in range(nb):
        seg = slice(b * tb, (b + 1) * tb)
        for h in range(N_HEADS):
            cs = slice(h * HEAD_DIM, (h + 1) * HEAD_DIM)
            qn = _rms_norm(mq_ref[seg, cs], qng_ref[...]).astype(BF16)
            s = lax.dot_general(qn, mk_ref[b, :, cs].astype(BF16), (((1,), (1,)), ((), ())),
                                preferred_element_type=F32) * SCALE
            e = jnp.exp(s - jnp.max(s, axis=-1, keepdims=True))
            p = e / jnp.sum(e, axis=-1, keepdims=True)
            o = jnp.dot(p.astype(BF16), mv_ref[b, :, cs].astype(BF16), preferred_element_type=F32)
            ym_ref[seg, cs] = (o * _silu(mg_ref[seg, cs])).astype(ym_ref.dtype)


def _branches(proj, left32, mk_arr, mv_arr, mem_blk0, p, *, nb, tb, n_seg_tiles, n_row_tiles):
    rows = nb * tb
    total = proj.shape[0]
    assert total == n_seg_tiles * n_row_tiles * rows and (nb == 1 or n_row_tiles == 1)
    assert rows % GMLP_CHUNK == 0
    n_mem = mk_arr.shape[1]
    conv_chunk = min(tb, 32)
    col = lambda u: pl.BlockSpec((rows, BRANCH_W), lambda s, i, u=u: (s * n_row_tiles + i, u))
    const = lambda shape: pl.BlockSpec(shape, lambda s, i: (0,) * len(shape))
    mem = pl.BlockSpec((nb, n_mem, BRANCH_W), lambda s, i: (mem_blk0 + s, 0, 0))
    in_specs = [col(U_AU), col(U_AV), col(U_AG), col(U_CA), col(U_CB), col(U_CG), col(U_MQ), col(U_MG),
                pl.BlockSpec((nb, HALO, BRANCH_W), lambda s, i: (s, 0, 0)), mem, mem,
                const((1, BRANCH_W)), const((1, BRANCH_W)),
                const((N_HEADS, GMLP_CHUNK, GMLP_CHUNK)), const((N_HEADS, GMLP_CHUNK, HEAD_DIM)),
                const((CONV_W, BRANCH_W)), const((1, BRANCH_W)), const((1, BRANCH_W)), const((1, BRANCH_W)),
                const((1, HEAD_DIM))]
    row_out = pl.BlockSpec((rows, BRANCH_W), lambda s, i: (s * n_row_tiles + i, 0))
    out_specs = [row_out, row_out, row_out, row_out,
                 pl.BlockSpec((nb, HALO, BRANCH_W), lambda s, i: (s, 0, 0))]
    out_shape = [jax.ShapeDtypeStruct((total, BRANCH_W), BF16)] * 3 + [
        jax.ShapeDtypeStruct((total, BRANCH_W), F32),
        jax.ShapeDtypeStruct((n_seg_tiles * nb, HALO, BRANCH_W), F32)]
    est = 2 * (8 * rows * BRANCH_W * 4 + 2 * nb * n_mem * BRANCH_W * 4 + 4 * rows * BRANCH_W * 4) + (4 << 20)
    return pl.pallas_call(
        functools.partial(_branch_kernel, nb=nb, tb=tb, conv_chunk=conv_chunk),
        grid=(n_seg_tiles, n_row_tiles),
        in_specs=in_specs,
        out_specs=out_specs,
        out_shape=out_shape,
        scratch_shapes=[pltpu.VMEM((rows, BRANCH_W), BF16), pltpu.VMEM((nb, HALO + tb, BRANCH_W), F32)],
        compiler_params=pltpu.CompilerParams(
            dimension_semantics=("arbitrary", "arbitrary"), vmem_limit_bytes=_vmem_limit(est)),
        name="branches",
    )(proj, proj, proj, proj, proj, proj, proj, proj, left32, mk_arr, mv_arr,
      p["gmlp_ln_g"], p["gmlp_ln_b"], p["mix_w"], p["mix_b"], p["conv_w"], p["conv_b"], p["conv_ln_g"],
      p["conv_ln_b"], p["q_norm_g"])


def _merge_kernel(ya_ref, yb_ref, yc_ref, ym_ref, gl0_ref, gl1_ref, gl2_ref, gl3_ref, bg_ref, wb_ref, wo_ref, x_ref,
                  o_ref, mix_scr, *, col_chunk):
    ys = (ya_ref, yb_ref, yc_ref, ym_ref)
    gls = (gl0_ref, gl1_ref, gl2_ref, gl3_ref)
    d = x_ref.shape[1]
    for c in range(d // col_chunk):
        cs = slice(c * col_chunk, (c + 1) * col_chunk)
        mixed = None
        for n in range(N_BRANCH):
            t = jnp.dot(ys[n][...], wb_ref[n, :, cs], preferred_element_type=F32)
            t = _sigmoid(gls[n][:, cs] + bg_ref[n:n + 1, cs]) * t
            mixed = t if mixed is None else mixed + t
        mix_scr[:, cs] = mixed.astype(BF16)
    o_ref[...] = x_ref[...] + jnp.dot(mix_scr[...], wo_ref[...], preferred_element_type=F32)


def _merge(ya, yb, yc, ym, proj, b_gate, wb_bf, wo_bf, x, *, tm):
    rows, d = x.shape
    tm = min(tm, rows)
    assert rows % tm == 0 and (U_GATES * BRANCH_W) % d == 0
    gate0 = U_GATES * BRANCH_W // d
    ysp = pl.BlockSpec((tm, BRANCH_W), lambda i: (i, 0))
    gl = lambda n: pl.BlockSpec((tm, d), lambda i, n=n: (i, gate0 + n))
    resident = lambda shape: pl.BlockSpec(shape, lambda i: (0,) * len(shape), pipeline_mode=pl.Buffered(1))
    est = (N_BRANCH * BRANCH_W * d * 2 + d * d * 2 + 2 * (4 * tm * BRANCH_W * 2 + 4 * tm * d * 4 + 2 * tm * d * 4)
           + tm * d * 2 + 4 * tm * d * 4)
    return pl.pallas_call(
        functools.partial(_merge_kernel, col_chunk=BRANCH_W),
        grid=(rows // tm,),
        in_specs=[ysp, ysp, ysp, ysp, gl(0), gl(1), gl(2), gl(3),
                  resident((N_BRANCH, d)), resident((N_BRANCH, BRANCH_W, d)), resident((d, d)),
                  pl.BlockSpec((tm, d), lambda i: (i, 0))],
        out_specs=pl.BlockSpec((tm, d), lambda i: (i, 0)),
        out_shape=jax.ShapeDtypeStruct((rows, d), F32),
        scratch_shapes=[pltpu.VMEM((tm, d), BF16)],
        compiler_params=pltpu.CompilerParams(
            dimension_semantics=("arbitrary",), vmem_limit_bytes=_vmem_limit(est)),
        name="merge",
    )(ya, yb, yc, ym, proj, proj, proj, proj, b_gate, wb_bf, wo_bf, x)


def _gmlp_mask():
    i = jnp.arange(GMLP_CHUNK)
    return (i[None, :] // CHUNK) <= (i[:, None] // CHUNK)


def kernel(x_prompt, x_sample, mem_prompt, cache_sb_k, cache_sb_v, state_conv, cache_mem_k, cache_mem_v, norm_g, w_in, gmlp_ln_g, gmlp_ln_b, gmlp_ws, gmlp_bs, conv_w, conv_b, conv_ln_g, conv_ln_b, mem_norm_g, w_mem_kv, q_norm_g, k_norm_g, b_gate, w_branch, w_out):
    n_b, seq, d = x_prompt.shape
    dec_b, dec_t, _ = x_sample.shape
    depth = w_in.shape[0]
    n_mem = mem_prompt.shape[1]
    past_len = cache_sb_k.shape[2]
    assert seq % 256 == 0 and past_len % HEAD_DIM == 0 and dec_b * dec_t == GMLP_CHUNK and dec_t <= HALO
    assert GMLP_CHUNK % dec_t == 0 and d == N_BRANCH * BRANCH_W
    assert seq % (SB_STREAMS_SHARED * HEAD_DIM) == 0 and dec_b % SB_STREAMS_OWN == 0

    w_in_bf = w_in.astype(BF16)
    w_mem_bf = w_mem_kv.astype(BF16)
    wb_bf = w_branch.astype(BF16)
    wo_bf = w_out.astype(BF16)

    ws_m = jnp.where(_gmlp_mask()[None, None], gmlp_ws, 0.0)
    mix_w_p = ws_m.astype(BF16)
    mix_b_p = jnp.broadcast_to(gmlp_bs[..., None], gmlp_bs.shape + (HEAD_DIM,))
    eye = jnp.eye(dec_b, dtype=F32)
    corner = ws_m[:, :, :dec_t, :dec_t]
    mix_w_s = jnp.einsum("bc,lgij->lgbicj", eye, corner).reshape(depth, N_HEADS, GMLP_CHUNK, GMLP_CHUNK).astype(BF16)
    mix_b_s = jnp.broadcast_to(jnp.tile(gmlp_bs[:, :, :dec_t], (1, 1, dec_b))[..., None],
                               gmlp_bs.shape + (HEAD_DIM,))

    cache_k2 = cache_sb_k.reshape(depth * dec_b * past_len, BRANCH_W)
    cache_v2 = cache_sb_v.reshape(depth * dec_b * past_len, BRANCH_W)
    cache_mk = cache_mem_k.reshape(depth * dec_b, n_mem, BRANCH_W)
    cache_mv = cache_mem_v.reshape(depth * dec_b, n_mem, BRANCH_W)
    state32 = jnp.pad(state_conv, ((0, 0), (0, 0), (HALO - (CONV_W - 1), 0), (0, 0)))
    zero32 = jnp.zeros((n_b, HALO, BRANCH_W), F32)

    xp = x_prompt.reshape(n_b * seq, d)
    xs = x_sample.reshape(dec_b * dec_t, d)
    tb_p = 256
    outs = {k: [] for k in ("kp", "vp", "cp", "mkp", "mvp", "ks", "vs", "cs", "gs")}
    for l in range(depth):
        common = dict(gmlp_ln_g=gmlp_ln_g[l][None], gmlp_ln_b=gmlp_ln_b[l][None], conv_w=conv_w[l],
                      conv_b=conv_b[l][None], conv_ln_g=conv_ln_g[l][None], conv_ln_b=conv_ln_b[l][None],
                      q_norm_g=q_norm_g[l][None])

        mk, mv = _memkv(mem_prompt.reshape(n_b * n_mem, d), mem_norm_g[l], w_mem_bf[l], k_norm_g[l])
        mk = mk.reshape(n_b, n_mem, BRANCH_W)
        mv = mv.reshape(n_b, n_mem, BRANCH_W)
        proj, kf, vf, kb, vb = _inproj(xp, norm_g[l], w_in_bf[l], tm=1024, tn=1024)
        yb = _sb_attn(proj, U_BQ, proj, U_BG, kb, 0, vb, 0, kb, vb, 0, n_groups=n_b,
                      n_steps=seq // (SB_STREAMS_SHARED * HEAD_DIM), n_streams=SB_STREAMS_SHARED, past_rows=seq,
                      shared_past=True)
        ya, yc, ym, _, nconv = _branches(
            proj, zero32, mk, mv, 0, dict(common, mix_w=mix_w_p[l], mix_b=mix_b_p[l]),
            nb=1, tb=tb_p, n_seg_tiles=n_b, n_row_tiles=seq // tb_p)
        xp = _merge(ya, yb, yc, ym, proj, b_gate[l], wb_bf[l], wo_bf[l], xp, tm=256)
        outs["kp"].append(kf.reshape(n_b, seq, N_HEADS, HEAD_DIM))
        outs["vp"].append(vf.reshape(n_b, seq, N_HEADS, HEAD_DIM))
        outs["cp"].append(nconv[:, HALO - (CONV_W - 1):])
        outs["mkp"].append(mk.reshape(n_b, n_mem, N_HEADS, HEAD_DIM))
        outs["mvp"].append(mv.reshape(n_b, n_mem, N_HEADS, HEAD_DIM))

        proj_s, kf_s, vf_s, _, _ = _inproj(xs, norm_g[l], w_in_bf[l], tm=GMLP_CHUNK, tn=1024)
        qkvg = proj_s[:, U_BQ * BRANCH_W:(U_BG + 1) * BRANCH_W].reshape(dec_b, dec_t, 4 * BRANCH_W)
        qkvg = jnp.pad(qkvg, ((0, 0), (0, HEAD_DIM - dec_t), (0, 0))).reshape(dec_b * HEAD_DIM, 4 * BRANCH_W)
        yb_s = _sb_attn(qkvg, 0, qkvg, 3, qkvg, 1, qkvg, 2, cache_k2, cache_v2, l * dec_b,
                        n_groups=dec_b // SB_STREAMS_OWN, n_steps=1, n_streams=SB_STREAMS_OWN, past_rows=past_len,
                        shared_past=False)
        yb_s = yb_s.reshape(dec_b, HEAD_DIM, BRANCH_W)[:, :dec_t].reshape(dec_b * dec_t, BRANCH_W)
        ya_s, yc_s, ym_s, vn_s, nconv_s = _branches(
            proj_s, state32[l], cache_mk, cache_mv, l, dict(common, mix_w=mix_w_s[l], mix_b=mix_b_s[l]),
            nb=dec_b, tb=dec_t, n_seg_tiles=1, n_row_tiles=1)
        xs = _merge(ya_s, yb_s, yc_s, ym_s, proj_s, b_gate[l], wb_bf[l], wo_bf[l], xs, tm=GMLP_CHUNK)
        outs["ks"].append(kf_s.reshape(dec_b, dec_t, N_HEADS, HEAD_DIM))
        outs["vs"].append(vf_s.reshape(dec_b, dec_t, N_HEADS, HEAD_DIM))
        outs["cs"].append(nconv_s[:, HALO - (CONV_W - 1):])
        outs["gs"].append(vn_s.reshape(dec_b, dec_t, BRANCH_W))

    st = lambda k: jnp.stack(outs[k])
    return (xp.reshape(n_b, seq, d), xs.reshape(dec_b, dec_t, d), st("kp"), st("vp"), st("cp"), st("mkp"), st("mvp"),
            st("ks"), st("vs"), st("cs"), st("gs"))
```

```python
import functools

import jax
import jax.numpy as jnp
from jax import lax
from jax.experimental import pallas as pl
from jax.experimental.pallas import tpu as pltpu

F32 = jnp.float32
BF16 = jnp.bfloat16

HEAD_DIM = 128
N_HEADS = 4
BRANCH_W = N_HEADS * HEAD_DIM
N_BRANCH = 4
GMLP_CHUNK = 128
CHUNK = 64
CONV_W = 31
SUBLANES = 8
HALO = 32
EPS = 1e-6
SCALE = HEAD_DIM ** -0.5
U_AU, U_AV, U_AG, U_BQ, U_BK, U_BV, U_BG, U_CA, U_CB, U_CG, U_MQ, U_MG, U_GATES = (
    0, 1, 2, 3, 4, 5, 6, 7, 8, 9, 10, 11, 12)
SB_EXIT = -110.0
SB_DONE = -1e30
SB_STREAMS_SHARED = 4
SB_STREAMS_OWN = 2
V7X_VMEM_BYTES = 64 * 1024 * 1024


def _vmem_limit(estimate_bytes):
    return int(min(max(estimate_bytes * 5 // 4 + (4 << 20), 16 << 20), V7X_VMEM_BYTES - (6 << 20)))


def _layer_spec(arr, l, **kw):
    shape = arr.shape[1:]
    return pl.BlockSpec((None,) + shape, lambda *_: (l,) + (0,) * len(shape), **kw)


def _sigmoid(x):
    return 1.0 / (1.0 + jnp.exp(-x))


def _silu(x):
    return x * _sigmoid(x)


def _layer_norm(x, g, b):
    mu = jnp.mean(x, axis=-1, keepdims=True)
    xc = x - mu
    return xc * lax.rsqrt(jnp.mean(xc * xc, axis=-1, keepdims=True) + EPS) * g + b


def _rms_norm(x, g):
    return x * lax.rsqrt(jnp.mean(x * x, axis=-1, keepdims=True) + EPS) * g


def _norm_rows(x_ref, g_ref, h_scr, row_chunk):
    def body(c, carry):
        r0 = pl.multiple_of(c * row_chunk, row_chunk)
        h_scr[pl.ds(r0, row_chunk), :] = _rms_norm(x_ref[pl.ds(r0, row_chunk), :], g_ref[...]).astype(BF16)
        return carry
    lax.fori_loop(0, x_ref.shape[0] // row_chunk, body, 0)


def _inproj_kernel(x_ref, g_ref, w_ref, o_ref, kf_ref, vf_ref, kb_ref, vb_ref, h_scr, *, kv_tile, row_chunk):
    j = pl.program_id(1)
    tm = x_ref.shape[0]

    @pl.when(j == 0)
    def _():
        _norm_rows(x_ref, g_ref, h_scr, row_chunk)

    acc = jnp.dot(h_scr[...], w_ref[...], preferred_element_type=F32)
    o_ref[...] = acc

    @pl.when(j == kv_tile)
    def _():
        kb_ref[...] = acc[:, :BRANCH_W].astype(BF16)
        vb_ref[...] = acc[:, BRANCH_W:].astype(BF16)
        for h in range(N_HEADS):
            kf_ref[pl.ds(h, tm, stride=N_HEADS), :] = acc[:, h * HEAD_DIM:(h + 1) * HEAD_DIM]
            vf_ref[pl.ds(h, tm, stride=N_HEADS), :] = acc[:, BRANCH_W + h * HEAD_DIM:BRANCH_W + (h + 1) * HEAD_DIM]


def _inproj(x, g_all, w_all, l, *, tm, tn):
    rows, d = x.shape
    n = w_all.shape[2]
    tm = min(tm, rows)
    assert rows % tm == 0 and n % tn == 0 and tn == 2 * BRANCH_W and (U_BK * BRANCH_W) % tn == 0 and U_BV == U_BK + 1
    kv_tile = U_BK * BRANCH_W // tn
    side = pl.BlockSpec((tm, BRANCH_W), lambda i, j: (i, 0))
    heads = pl.BlockSpec((tm * N_HEADS, HEAD_DIM), lambda i, j: (i, 0))
    est = 2 * tm * d * 4 + tm * d * 2 + 2 * d * tn * 2 + 3 * tm * tn * 4 + 2 * tm * BRANCH_W * 12
    return pl.pallas_call(
        functools.partial(_inproj_kernel, kv_tile=kv_tile, row_chunk=min(tm, 32)),
        grid=(rows // tm, n // tn),
        in_specs=[pl.BlockSpec((tm, d), lambda i, j: (i, 0)),
                  _layer_spec(g_all, l),
                  pl.BlockSpec((None, d, tn), lambda i, j: (l, 0, j))],
        out_specs=[pl.BlockSpec((tm, tn), lambda i, j: (i, j)), heads, heads, side, side],
        out_shape=[jax.ShapeDtypeStruct((rows, n), F32),
                   jax.ShapeDtypeStruct((rows * N_HEADS, HEAD_DIM), F32),
                   jax.ShapeDtypeStruct((rows * N_HEADS, HEAD_DIM), F32),
                   jax.ShapeDtypeStruct((rows, BRANCH_W), BF16), jax.ShapeDtypeStruct((rows, BRANCH_W), BF16)],
        scratch_shapes=[pltpu.VMEM((tm, d), BF16)],
        compiler_params=pltpu.CompilerParams(
            dimension_semantics=("arbitrary", "arbitrary"), vmem_limit_bytes=_vmem_limit(est)),
        name="inproj",
    )(x, g_all, w_all)


def _memkv_kernel(x_ref, g_ref, w_ref, kg_ref, mk_ref, mv_ref, h_scr, *, row_chunk):
    j = pl.program_id(0)

    @pl.when(j == 0)
    def _():
        _norm_rows(x_ref, g_ref, h_scr, row_chunk)

    acc = jnp.dot(h_scr[...], w_ref[...], preferred_element_type=F32)

    @pl.when(j == 0)
    def _():
        for hh in range(N_HEADS):
            cs = slice(hh * HEAD_DIM, (hh + 1) * HEAD_DIM)
            mk_ref[:, cs] = _rms_norm(acc[:, cs], kg_ref[...])

    @pl.when(j == 1)
    def _():
        mv_ref[...] = acc


def _memkv(mem, g_all, w_all, kg_all, l):
    rows, d = mem.shape
    assert w_all.shape[2] == 2 * BRANCH_W
    whole = lambda shape: pl.BlockSpec(shape, lambda j: (0,) * len(shape))
    est = 2 * rows * d * 4 + rows * d * 2 + 2 * d * BRANCH_W * 2 + 6 * rows * BRANCH_W * 4
    return pl.pallas_call(
        functools.partial(_memkv_kernel, row_chunk=min(rows, 32)),
        grid=(2,),
        in_specs=[whole((rows, d)), _layer_spec(g_all, l), pl.BlockSpec((None, d, BRANCH_W), lambda j: (l, 0, j)),
                  _layer_spec(kg_all, l)],
        out_specs=[whole((rows, BRANCH_W)), whole((rows, BRANCH_W))],
        out_shape=[jax.ShapeDtypeStruct((rows, BRANCH_W), F32)] * 2,
        scratch_shapes=[pltpu.VMEM((rows, d), BF16)],
        compiler_params=pltpu.CompilerParams(
            dimension_semantics=("arbitrary",), vmem_limit_bytes=_vmem_limit(est)),
        name="memkv",
    )(mem, g_all, w_all, kg_all)


def _sb_kernel(q_ref, g_ref, kd_ref, vd_ref, kp_ref, vp_ref, o_ref, *, n_streams, q_rows, shared_past, past_rows):
    bq = HEAD_DIM
    step = pl.program_id(1)
    heads = [slice(h * HEAD_DIM, (h + 1) * HEAD_DIM) for h in range(N_HEADS)]
    chains = [(s, h) for s in range(n_streams) for h in range(N_HEADS)]
    if shared_past:
        assert q_rows == bq
        n_past = [step * n_streams + s for s in range(n_streams)]
        row0 = [s * bq for s in range(n_streams)]
    else:
        n_past = [past_rows // bq] * n_streams
        row0 = [pl.multiple_of((step * n_streams + s) * q_rows, q_rows) for s in range(n_streams)]

    def new_rows(ref, s, h):
        x = ref[pl.ds(row0[s], q_rows), heads[h]]
        if q_rows < bq:
            x = jnp.concatenate([x, jnp.zeros((bq - q_rows, HEAD_DIM), x.dtype)], axis=0)
        return x.astype(BF16)

    def past_rows_of(ref, s, h, kb):
        if shared_past:
            return ref[pl.ds(pl.multiple_of(kb * bq, bq), bq), heads[h]].astype(BF16)
        first = (s * past_rows + kb * bq) * N_HEADS + h
        return ref[pl.ds(first, bq, stride=N_HEADS), :].astype(BF16)

    q = [new_rows(q_ref, s, h) for s, h in chains]

    jj = lax.broadcasted_iota(jnp.int32, (2 * bq, 2 * bq), 0) & (bq - 1)
    ss = lax.broadcasted_iota(jnp.int32, (2 * bq, 2 * bq), 1)
    tri = jnp.where(ss >= bq, 1.0, jnp.where(jj > ss, 1.0, 0.0)).astype(BF16)
    row = lax.broadcasted_iota(jnp.int32, (bq, bq), 0)
    col = lax.broadcasted_iota(jnp.int32, (bq, bq), 1)
    causal = col < row

    def block(k_blks, v_blks, carry, acc, diag):
        zs, splits = [], []
        for c in range(len(chains)):
            z = lax.dot_general(q[c], k_blks[c], (((1,), (1,)), ((), ())), preferred_element_type=F32) * SCALE
            log_keep = -(jnp.maximum(z, 0.0) + jnp.log(1.0 + jnp.exp(-jnp.abs(z))))
            if diag:
                log_keep = jnp.where(causal, log_keep, 0.0)
            hi = log_keep.astype(BF16)
            lo = (log_keep - hi.astype(F32)).astype(BF16)
            zs.append(z + log_keep)
            splits.append(jnp.concatenate([hi, lo], axis=1))
        cum = jnp.dot(jnp.concatenate(splits, axis=0), tri, preferred_element_type=F32)
        new_carry, new_acc = [], []
        for c in range(len(chains)):
            cum_c = cum[c * bq:(c + 1) * bq]
            w = jnp.exp(zs[c] + carry[c] + cum_c[:, :bq])
            if diag:
                w = jnp.where(causal, w, 0.0)
            new_acc.append(acc[c] + jnp.dot(w.astype(BF16), v_blks[c], preferred_element_type=F32))
            new_carry.append(carry[c] + cum_c[:, bq:])
        return new_carry, new_acc

    def retire(carry, n_left):
        return tuple(jnp.where(n_left[s] > 0, carry[c], SB_DONE) for c, (s, h) in enumerate(chains))

    def top_of(carry):
        top = carry[0]
        for c in carry[1:]:
            top = jnp.maximum(top, c)
        return jnp.max(top)

    zeros = [jnp.zeros((bq, HEAD_DIM), F32)] * len(chains)
    carry, acc = block([new_rows(kd_ref, s, h) for s, h in chains], [new_rows(vd_ref, s, h) for s, h in chains],
                       zeros, zeros, True)
    carry = retire(carry, n_past)
    longest = n_past[0]
    for n in n_past[1:]:
        longest = jnp.maximum(longest, n)

    def cond(state):
        t, _, _, top = state
        return jnp.logical_and(t < longest, top > SB_EXIT)

    def body(state):
        t, carry, acc, _ = state
        kbs = [jnp.maximum(n_past[s] - 1 - t, 0) for s in range(n_streams)]
        carry, acc = block([past_rows_of(kp_ref, s, h, kbs[s]) for s, h in chains],
                           [past_rows_of(vp_ref, s, h, kbs[s]) for s, h in chains], carry, acc, False)
        carry = retire(carry, [n - 1 - t for n in n_past])
        return t + 1, carry, tuple(acc), top_of(carry)

    _, _, acc, _ = lax.while_loop(cond, body, (jnp.int32(0), carry, tuple(acc), top_of(carry)))
    for c, (s, h) in enumerate(chains):
        rows = pl.ds(row0[s], q_rows)
        o_ref[rows, heads[h]] = (acc[c][:q_rows] * _silu(g_ref[rows, heads[h]])).astype(o_ref.dtype)


def _sb_attn(q_arr, q_u, g_arr, g_u, kd_arr, kd_u, vd_arr, vd_u, past_k, past_v, past_blk0, *, n_groups, n_steps,
             n_streams, q_rows, past_rows, shared_past):
    if shared_past:
        rows = n_streams * HEAD_DIM
        blk = lambda u: pl.BlockSpec((rows, BRANCH_W), lambda g, i, u=u: (g * n_steps + i, u))
        past = pl.BlockSpec((past_rows, BRANCH_W), lambda g, i: (past_blk0 + g, 0), pipeline_mode=pl.Buffered(1))
        past_bytes = 2 * past_rows * BRANCH_W * past_k.dtype.itemsize
        out_rows = n_groups * n_steps * rows
        out_spec = pl.BlockSpec((rows, BRANCH_W), lambda g, i: (g * n_steps + i, 0))
    else:
        assert n_groups == 1 and past_blk0 % n_streams == 0
        rows = n_steps * n_streams * q_rows
        blk = lambda u: pl.BlockSpec((rows, BRANCH_W), lambda g, i, u=u: (0, u))
        past = pl.BlockSpec((n_streams * past_rows * N_HEADS, HEAD_DIM),
                            lambda g, i: (past_blk0 // n_streams + i, 0))
        past_bytes = 4 * n_streams * past_rows * BRANCH_W * past_k.dtype.itemsize
        out_rows = rows
        out_spec = pl.BlockSpec((rows, BRANCH_W), lambda g, i: (0, 0))
    est = past_bytes + 16 * rows * BRANCH_W * 4 + (4 << 20)
    return pl.pallas_call(
        functools.partial(_sb_kernel, n_streams=n_streams, q_rows=q_rows, shared_past=shared_past,
                          past_rows=past_rows),
        grid=(n_groups, n_steps),
        in_specs=[blk(q_u), blk(g_u), blk(kd_u), blk(vd_u), past, past],
        out_specs=out_spec,
        out_shape=jax.ShapeDtypeStruct((out_rows, BRANCH_W), BF16),
        compiler_params=pltpu.CompilerParams(
            dimension_semantics=("arbitrary", "arbitrary"), vmem_limit_bytes=_vmem_limit(est)),
        name="sb_attn",
    )(q_arr, g_arr, kd_arr, vd_arr, past_k, past_v)


def _branch_kernel(au_ref, av_ref, ag_ref, ca_ref, cb_ref, cg_ref, mq_ref, mg_ref, left_ref, mk_ref, mv_ref,
                   lng_ref, lnb_ref, wm_ref, pb_ref, cw_ref, cbias_ref, clng_ref, clnb_ref, qng_ref,
                   ya_ref, yc_ref, ym_ref, vn_ref, nconv_ref, vn_scr, xp_scr, sh_scr, *, nb, tb, conv_chunk):
    rows = nb * tb

    vn = _layer_norm(av_ref[...], lng_ref[...], lnb_ref[...])
    vn_ref[...] = vn
    vn_scr[...] = vn.astype(BF16)
    for c in range(rows // GMLP_CHUNK):
        rs = slice(c * GMLP_CHUNK, (c + 1) * GMLP_CHUNK)
        for g in range(N_HEADS):
            cs = slice(g * HEAD_DIM, (g + 1) * HEAD_DIM)
            s = jnp.dot(wm_ref[g], vn_scr[rs, cs], preferred_element_type=F32) + pb_ref[g]
            ya_ref[rs, cs] = (au_ref[rs, cs] * s * _silu(ag_ref[rs, cs])).astype(ya_ref.dtype)

    first_tap = HALO - (CONV_W - 1)
    sh_rows = sh_scr.shape[1]
    first = pl.program_id(1) == 0
    for b in range(nb):
        seg = slice(b * tb, (b + 1) * tb)

        @pl.when(first)
        def _():
            xp_scr[b, 0:HALO, :] = left_ref[b]

        xp_scr[b, HALO:HALO + tb, :] = ca_ref[seg, :] * _sigmoid(cb_ref[seg, :])
        for r in range(1, SUBLANES):
            sh_scr[r - 1] = xp_scr[b, r:r + sh_rows, :]
        for c in range(tb // conv_chunk):
            r0 = c * conv_chunk
            acc = jnp.broadcast_to(cbias_ref[...], (conv_chunk, BRANCH_W))
            for w in range(CONV_W):
                shift, base = (first_tap + w) % SUBLANES, (first_tap + w) // SUBLANES * SUBLANES
                if shift == 0:
                    taps = xp_scr[b, r0 + base:r0 + base + conv_chunk, :]
                else:
                    taps = sh_scr[shift - 1, r0 + base:r0 + base + conv_chunk, :]
                acc = acc + taps * cw_ref[w:w + 1, :]
            cn = _layer_norm(acc, clng_ref[...], clnb_ref[...])
            rs = slice(b * tb + r0, b * tb + r0 + conv_chunk)
            yc_ref[rs, :] = (_silu(cn) * _silu(cg_ref[rs, :])).astype(yc_ref.dtype)
        tail = xp_scr[b, tb:tb + HALO, :]
        nconv_ref[b] = tail
        xp_scr[b, 0:HALO, :] = tail

    for b in range(nb):
        seg = slice(b * tb, (b + 1) * tb)
        for h in range(N_HEADS):
            cs = slice(h * HEAD_DIM, (h + 1) * HEAD_DIM)
            qn = _rms_norm(mq_ref[seg, cs], qng_ref[...]).astype(BF16)
            s = lax.dot_general(qn, mk_ref[b, :, cs].astype(BF16), (((1,), (1,)), ((), ())),
                                preferred_element_type=F32) * SCALE
            e = jnp.exp(s - jnp.max(s, axis=-1, keepdims=True))
            p = e / jnp.sum(e, axis=-1, keepdims=True)
            o = jnp.dot(p.astype(BF16), mv_ref[b, :, cs].astype(BF16), preferred_element_type=F32)
            ym_ref[seg, cs] = (o * _silu(mg_ref[seg, cs])).astype(ym_ref.dtype)


def _branches(proj, left32, mk_arr, mv_arr, seg_blk0, p, l, *, nb, tb, n_seg_tiles, n_row_tiles):
    rows = nb * tb
    total = proj.shape[0]
    assert total == n_seg_tiles * n_row_tiles * rows and (nb == 1 or n_row_tiles == 1)
    assert rows % GMLP_CHUNK == 0 and tb % SUBLANES == 0
    n_mem = mk_arr.shape[1]
    conv_chunk = min(tb, 32)
    sh_rows = tb + HALO - SUBLANES
    col = lambda u: pl.BlockSpec((rows, BRANCH_W), lambda s, i, u=u: (s * n_row_tiles + i, u))
    mem = pl.BlockSpec((nb, n_mem, BRANCH_W), lambda s, i: (seg_blk0 + s, 0, 0))
    names = ("gmlp_ln_g", "gmlp_ln_b", "mix_w", "mix_b", "conv_w", "conv_b", "conv_ln_g", "conv_ln_b", "q_norm_g")
    in_specs = [col(U_AU), col(U_AV), col(U_AG), col(U_CA), col(U_CB), col(U_CG), col(U_MQ), col(U_MG),
                pl.BlockSpec((nb, HALO, BRANCH_W), lambda s, i: (seg_blk0 + s, 0, 0)), mem, mem]
    in_specs += [_layer_spec(p[k], l) for k in names]
    row_out = pl.BlockSpec((rows, BRANCH_W), lambda s, i: (s * n_row_tiles + i, 0))
    out_specs = [row_out, row_out, row_out, row_out,
                 pl.BlockSpec((nb, HALO, BRANCH_W), lambda s, i: (s, 0, 0))]
    out_shape = [jax.ShapeDtypeStruct((total, BRANCH_W), BF16)] * 3 + [
        jax.ShapeDtypeStruct((total, BRANCH_W), F32),
        jax.ShapeDtypeStruct((n_seg_tiles * nb, HALO, BRANCH_W), F32)]
    est = (2 * (8 * rows * BRANCH_W * 4 + 2 * nb * n_mem * BRANCH_W * 4 + 4 * rows * BRANCH_W * 4)
           + SUBLANES * (tb + HALO) * BRANCH_W * 4 + (4 << 20))
    return pl.pallas_call(
        functools.partial(_branch_kernel, nb=nb, tb=tb, conv_chunk=conv_chunk),
        grid=(n_seg_tiles, n_row_tiles),
        in_specs=in_specs,
        out_specs=out_specs,
        out_shape=out_shape,
        scratch_shapes=[pltpu.VMEM((rows, BRANCH_W), BF16), pltpu.VMEM((nb, HALO + tb, BRANCH_W), F32),
                        pltpu.VMEM((SUBLANES - 1, sh_rows, BRANCH_W), F32)],
        compiler_params=pltpu.CompilerParams(
            dimension_semantics=("arbitrary", "arbitrary"), vmem_limit_bytes=_vmem_limit(est)),
        name="branches",
    )(proj, proj, proj, proj, proj, proj, proj, proj, left32, mk_arr, mv_arr, *[p[k] for k in names])


def _merge_kernel(ya_ref, yb_ref, yc_ref, ym_ref, gl0_ref, gl1_ref, gl2_ref, gl3_ref, bg_ref, wb_ref, wo_ref, x_ref,
                  o_ref, mix_scr, *, col_chunk):
    ys = (ya_ref, yb_ref, yc_ref, ym_ref)
    gls = (gl0_ref, gl1_ref, gl2_ref, gl3_ref)
    d = x_ref.shape[1]
    for c in range(d // col_chunk):
        cs = slice(c * col_chunk, (c + 1) * col_chunk)
        mixed = None
        for n in range(N_BRANCH):
            t = jnp.dot(ys[n][...], wb_ref[n, :, cs], preferred_element_type=F32)
            t = _sigmoid(gls[n][:, cs] + bg_ref[n:n + 1, cs]) * t
            mixed = t if mixed is None else mixed + t
        mix_scr[:, cs] = mixed.astype(BF16)
    o_ref[...] = x_ref[...] + jnp.dot(mix_scr[...], wo_ref[...], preferred_element_type=F32)


def _merge(ya, yb, yc, ym, proj, bg_all, wb_all, wo_all, l, x, *, tm):
    rows, d = x.shape
    tm = min(tm, rows)
    assert rows % tm == 0 and (U_GATES * BRANCH_W) % d == 0
    gate0 = U_GATES * BRANCH_W // d
    ysp = pl.BlockSpec((tm, BRANCH_W), lambda i: (i, 0))
    gl = lambda n: pl.BlockSpec((tm, d), lambda i, n=n: (i, gate0 + n))
    est = (N_BRANCH * BRANCH_W * d * 2 + d * d * 2 + 2 * (4 * tm * BRANCH_W * 2 + 4 * tm * d * 4 + 2 * tm * d * 4)
           + tm * d * 2 + 4 * tm * d * 4)
    return pl.pallas_call(
        functools.partial(_merge_kernel, col_chunk=BRANCH_W),
        grid=(rows // tm,),
        in_specs=[ysp, ysp, ysp, ysp, gl(0), gl(1), gl(2), gl(3),
                  _layer_spec(bg_all, l, pipeline_mode=pl.Buffered(1)),
                  _layer_spec(wb_all, l, pipeline_mode=pl.Buffered(1)),
                  _layer_spec(wo_all, l, pipeline_mode=pl.Buffered(1)),
                  pl.BlockSpec((tm, d), lambda i: (i, 0))],
        out_specs=pl.BlockSpec((tm, d), lambda i: (i, 0)),
        out_shape=jax.ShapeDtypeStruct((rows, d), F32),
        scratch_shapes=[pltpu.VMEM((tm, d), BF16)],
        compiler_params=pltpu.CompilerParams(
            dimension_semantics=("arbitrary",), vmem_limit_bytes=_vmem_limit(est)),
        name="merge",
    )(ya, yb, yc, ym, proj, proj, proj, proj, bg_all, wb_all, wo_all, x)


def _gmlp_mask():
    i = jnp.arange(GMLP_CHUNK)
    return (i[None, :] // CHUNK) <= (i[:, None] // CHUNK)


def kernel(x_prompt, x_sample, mem_prompt, cache_sb_k, cache_sb_v, state_conv, cache_mem_k, cache_mem_v, norm_g, w_in, gmlp_ln_g, gmlp_ln_b, gmlp_ws, gmlp_bs, conv_w, conv_b, conv_ln_g, conv_ln_b, mem_norm_g, w_mem_kv, q_norm_g, k_norm_g, b_gate, w_branch, w_out):
    n_b, seq, d = x_prompt.shape
    dec_b, dec_t, _ = x_sample.shape
    depth = w_in.shape[0]
    n_mem = mem_prompt.shape[1]
    past_len = cache_sb_k.shape[2]
    assert seq % 256 == 0 and past_len % HEAD_DIM == 0 and dec_b * dec_t == GMLP_CHUNK and dec_t <= HALO
    assert GMLP_CHUNK % dec_t == 0 and d == N_BRANCH * BRANCH_W and dec_t % SUBLANES == 0
    assert seq % (SB_STREAMS_SHARED * HEAD_DIM) == 0 and dec_b % SB_STREAMS_OWN == 0

    w_in_bf = w_in.astype(BF16)
    w_mem_bf = w_mem_kv.astype(BF16)
    wb_bf = w_branch.astype(BF16)
    wo_bf = w_out.astype(BF16)
    row = lambda a: a.reshape(depth, 1, a.shape[-1])

    ws_m = jnp.where(_gmlp_mask()[None, None], gmlp_ws, 0.0)
    eye = jnp.eye(dec_b, dtype=F32)
    corner = ws_m[:, :, :dec_t, :dec_t]
    mix_w_s = jnp.einsum("bc,lgij->lgbicj", eye, corner).reshape(depth, N_HEADS, GMLP_CHUNK, GMLP_CHUNK)
    mix_b_s = jnp.tile(gmlp_bs[:, :, :dec_t], (1, 1, dec_b))
    lane_bcast = lambda a: jnp.broadcast_to(a[..., None], a.shape + (HEAD_DIM,))
    shared = dict(gmlp_ln_g=row(gmlp_ln_g), gmlp_ln_b=row(gmlp_ln_b), conv_w=conv_w, conv_b=row(conv_b),
                  conv_ln_g=row(conv_ln_g), conv_ln_b=row(conv_ln_b), q_norm_g=row(q_norm_g))
    params_p = dict(shared, mix_w=ws_m.astype(BF16), mix_b=lane_bcast(gmlp_bs))
    params_s = dict(shared, mix_w=mix_w_s.astype(BF16), mix_b=lane_bcast(mix_b_s))
    norm_g3, mem_norm_g3, k_norm_g3 = row(norm_g), row(mem_norm_g), row(k_norm_g)

    cache_k2 = cache_sb_k.reshape(depth * dec_b * past_len * N_HEADS, HEAD_DIM)
    cache_v2 = cache_sb_v.reshape(depth * dec_b * past_len * N_HEADS, HEAD_DIM)
    cache_mk = cache_mem_k.reshape(depth * dec_b, n_mem, BRANCH_W)
    cache_mv = cache_mem_v.reshape(depth * dec_b, n_mem, BRANCH_W)
    state32 = jnp.pad(state_conv, ((0, 0), (0, 0), (HALO - (CONV_W - 1), 0), (0, 0))).reshape(
        depth * dec_b, HALO, BRANCH_W)
    zero32 = jnp.zeros((n_b, HALO, BRANCH_W), F32)

    xp = x_prompt.reshape(n_b * seq, d)
    xs = x_sample.reshape(dec_b * dec_t, d)
    mem2 = mem_prompt.reshape(n_b * n_mem, d)
    tb_p = 256
    outs = {k: [] for k in ("kp", "vp", "cp", "mkp", "mvp", "ks", "vs", "cs", "gs")}
    for l in range(depth):
        mk, mv = _memkv(mem2, mem_norm_g3, w_mem_bf, k_norm_g3, l)
        mk = mk.reshape(n_b, n_mem, BRANCH_W)
        mv = mv.reshape(n_b, n_mem, BRANCH_W)
        proj, kf, vf, kb, vb = _inproj(xp, norm_g3, w_in_bf, l, tm=1024, tn=1024)
        yb = _sb_attn(proj, U_BQ, proj, U_BG, kb, 0, vb, 0, kb, vb, 0, n_groups=n_b,
                      n_steps=seq // (SB_STREAMS_SHARED * HEAD_DIM), n_streams=SB_STREAMS_SHARED, q_rows=HEAD_DIM,
                      past_rows=seq, shared_past=True)
        ya, yc, ym, _, nconv = _branches(proj, zero32, mk, mv, 0, params_p, l,
                                         nb=1, tb=tb_p, n_seg_tiles=n_b, n_row_tiles=seq // tb_p)
        xp = _merge(ya, yb, yc, ym, proj, b_gate, wb_bf, wo_bf, l, xp, tm=256)
        outs["kp"].append(kf.reshape(n_b, seq, N_HEADS, HEAD_DIM))
        outs["vp"].append(vf.reshape(n_b, seq, N_HEADS, HEAD_DIM))
        outs["cp"].append(nconv[:, HALO - (CONV_W - 1):])
        outs["mkp"].append(mk.reshape(n_b, n_mem, N_HEADS, HEAD_DIM))
        outs["mvp"].append(mv.reshape(n_b, n_mem, N_HEADS, HEAD_DIM))

        proj_s, kf_s, vf_s, _, _ = _inproj(xs, norm_g3, w_in_bf, l, tm=GMLP_CHUNK, tn=1024)
        yb_s = _sb_attn(proj_s, U_BQ, proj_s, U_BG, proj_s, U_BK, proj_s, U_BV, cache_k2, cache_v2, l * dec_b,
                        n_groups=1, n_steps=dec_b // SB_STREAMS_OWN, n_streams=SB_STREAMS_OWN, q_rows=dec_t,
                        past_rows=past_len, shared_past=False)
        ya_s, yc_s, ym_s, vn_s, nconv_s = _branches(proj_s, state32, cache_mk, cache_mv, l, params_s, l,
                                                    nb=dec_b, tb=dec_t, n_seg_tiles=1, n_row_tiles=1)
        xs = _merge(ya_s, yb_s, yc_s, ym_s, proj_s, b_gate, wb_bf, wo_bf, l, xs, tm=GMLP_CHUNK)
        outs["ks"].append(kf_s.reshape(dec_b, dec_t, N_HEADS, HEAD_DIM))
        outs["vs"].append(vf_s.reshape(dec_b, dec_t, N_HEADS, HEAD_DIM))
        outs["cs"].append(nconv_s[:, HALO - (CONV_W - 1):])
        outs["gs"].append(vn_s.reshape(dec_b, dec_t, BRANCH_W))

    st = lambda k: jnp.stack(outs[k])
    return (xp.reshape(n_b, seq, d), xs.reshape(dec_b, dec_t, d), st("kp"), st("vp"), st("cp"), st("mkp"), st("mvp"),
            st("ks"), st("vs"), st("cs"), st("gs"))
```

```python
import functools

import jax
import jax.numpy as jnp
from jax import lax
from jax.experimental import pallas as pl
from jax.experimental.pallas import tpu as pltpu

F32 = jnp.float32
BF16 = jnp.bfloat16

HEAD_DIM = 128
N_HEADS = 4
BRANCH_W = N_HEADS * HEAD_DIM
N_BRANCH = 4
GMLP_CHUNK = 128
CHUNK = 64
CONV_W = 31
SUBLANES = 8
HALO = 32
EPS = 1e-6
SCALE = HEAD_DIM ** -0.5
U_AU, U_AV, U_AG, U_BQ, U_BK, U_BV, U_BG, U_CA, U_CB, U_CG, U_MQ, U_MG, U_GATES = (
    0, 1, 2, 3, 4, 5, 6, 7, 8, 9, 10, 11, 12)
SB_EXIT = -110.0
SB_DONE = -1e30
SB_STREAMS_SHARED = 4
SB_STREAMS_OWN = 2
V7X_VMEM_BYTES = 64 * 1024 * 1024


def _vmem_limit(estimate_bytes):
    return int(min(max(estimate_bytes * 5 // 4 + (4 << 20), 16 << 20), V7X_VMEM_BYTES - (6 << 20)))


def _layer_spec(arr, l, **kw):
    shape = arr.shape[1:]
    return pl.BlockSpec((None,) + shape, lambda *_: (l,) + (0,) * len(shape), **kw)


def _sigmoid(x):
    return 1.0 / (1.0 + jnp.exp(-x))


def _silu(x):
    return x * _sigmoid(x)


def _layer_norm(x, g, b):
    mu = jnp.mean(x, axis=-1, keepdims=True)
    xc = x - mu
    return xc * lax.rsqrt(jnp.mean(xc * xc, axis=-1, keepdims=True) + EPS) * g + b


def _rms_norm(x, g):
    return x * lax.rsqrt(jnp.mean(x * x, axis=-1, keepdims=True) + EPS) * g


def _norm_rows(x_ref, g_ref, h_scr, row_chunk):
    def body(c, carry):
        r0 = pl.multiple_of(c * row_chunk, row_chunk)
        h_scr[pl.ds(r0, row_chunk), :] = _rms_norm(x_ref[pl.ds(r0, row_chunk), :], g_ref[...]).astype(BF16)
        return carry
    lax.fori_loop(0, x_ref.shape[0] // row_chunk, body, 0)


def _inproj_kernel(x_ref, g_ref, w_ref, o_ref, gate_ref, kf_ref, vf_ref, kb_ref, vb_ref, h_scr, *, kv_tile, gate_tile0,
                   row_chunk):
    j = pl.program_id(1)
    tm = x_ref.shape[0]

    @pl.when(j == 0)
    def _():
        _norm_rows(x_ref, g_ref, h_scr, row_chunk)

    acc = jnp.dot(h_scr[...], w_ref[...], preferred_element_type=F32)

    @pl.when(j < gate_tile0)
    def _():
        o_ref[...] = acc

    @pl.when(j >= gate_tile0)
    def _():
        gate_ref[...] = acc.astype(BF16)

    @pl.when(j == kv_tile)
    def _():
        kb_ref[...] = acc[:, :BRANCH_W].astype(BF16)
        vb_ref[...] = acc[:, BRANCH_W:].astype(BF16)
        for h in range(N_HEADS):
            kf_ref[pl.ds(h, tm, stride=N_HEADS), :] = acc[:, h * HEAD_DIM:(h + 1) * HEAD_DIM]
            vf_ref[pl.ds(h, tm, stride=N_HEADS), :] = acc[:, BRANCH_W + h * HEAD_DIM:BRANCH_W + (h + 1) * HEAD_DIM]


def _inproj(x, g_all, w_all, l, *, tm, tn):
    rows, d = x.shape
    n = w_all.shape[2]
    tm = min(tm, rows)
    assert rows % tm == 0 and n % tn == 0 and tn == 2 * BRANCH_W and (U_BK * BRANCH_W) % tn == 0 and U_BV == U_BK + 1
    kv_tile = U_BK * BRANCH_W // tn
    assert (U_GATES * BRANCH_W) % tn == 0
    gate_tile0 = U_GATES * BRANCH_W // tn
    side = pl.BlockSpec((tm, BRANCH_W), lambda i, j: (i, 0))
    heads = pl.BlockSpec((tm * N_HEADS, HEAD_DIM), lambda i, j: (i, 0))
    est = 2 * tm * d * 4 + tm * d * 2 + 2 * d * tn * 2 + 3 * tm * tn * 4 + 2 * tm * tn * 2 + 2 * tm * BRANCH_W * 12
    return pl.pallas_call(
        functools.partial(_inproj_kernel, kv_tile=kv_tile, gate_tile0=gate_tile0, row_chunk=min(tm, 32)),
        grid=(rows // tm, n // tn),
        in_specs=[pl.BlockSpec((tm, d), lambda i, j: (i, 0)),
                  _layer_spec(g_all, l),
                  pl.BlockSpec((None, d, tn), lambda i, j: (l, 0, j))],
        out_specs=[pl.BlockSpec((tm, tn), lambda i, j: (i, jnp.minimum(j, gate_tile0 - 1))),
                   pl.BlockSpec((tm, tn), lambda i, j: (i, jnp.maximum(j - gate_tile0, 0))),
                   heads, heads, side, side],
        out_shape=[jax.ShapeDtypeStruct((rows, gate_tile0 * tn), F32),
                   jax.ShapeDtypeStruct((rows, n - gate_tile0 * tn), BF16),
                   jax.ShapeDtypeStruct((rows * N_HEADS, HEAD_DIM), F32),
                   jax.ShapeDtypeStruct((rows * N_HEADS, HEAD_DIM), F32),
                   jax.ShapeDtypeStruct((rows, BRANCH_W), BF16), jax.ShapeDtypeStruct((rows, BRANCH_W), BF16)],
        scratch_shapes=[pltpu.VMEM((tm, d), BF16)],
        compiler_params=pltpu.CompilerParams(
            dimension_semantics=("arbitrary", "arbitrary"), vmem_limit_bytes=_vmem_limit(est)),
        name="inproj",
    )(x, g_all, w_all)


def _memkv_kernel(x_ref, g_ref, w_ref, kg_ref, mk_ref, mv_ref, h_scr, *, row_chunk):
    j = pl.program_id(0)

    @pl.when(j == 0)
    def _():
        _norm_rows(x_ref, g_ref, h_scr, row_chunk)

    acc = jnp.dot(h_scr[...], w_ref[...], preferred_element_type=F32)

    @pl.when(j == 0)
    def _():
        for hh in range(N_HEADS):
            cs = slice(hh * HEAD_DIM, (hh + 1) * HEAD_DIM)
            mk_ref[:, cs] = _rms_norm(acc[:, cs], kg_ref[...])

    @pl.when(j == 1)
    def _():
        mv_ref[...] = acc


def _memkv(mem, g_all, w_all, kg_all, l):
    rows, d = mem.shape
    assert w_all.shape[2] == 2 * BRANCH_W
    whole = lambda shape: pl.BlockSpec(shape, lambda j: (0,) * len(shape))
    est = 2 * rows * d * 4 + rows * d * 2 + 2 * d * BRANCH_W * 2 + 6 * rows * BRANCH_W * 4
    return pl.pallas_call(
        functools.partial(_memkv_kernel, row_chunk=min(rows, 32)),
        grid=(2,),
        in_specs=[whole((rows, d)), _layer_spec(g_all, l), pl.BlockSpec((None, d, BRANCH_W), lambda j: (l, 0, j)),
                  _layer_spec(kg_all, l)],
        out_specs=[whole((rows, BRANCH_W)), whole((rows, BRANCH_W))],
        out_shape=[jax.ShapeDtypeStruct((rows, BRANCH_W), F32)] * 2,
        scratch_shapes=[pltpu.VMEM((rows, d), BF16)],
        compiler_params=pltpu.CompilerParams(
            dimension_semantics=("arbitrary",), vmem_limit_bytes=_vmem_limit(est)),
        name="memkv",
    )(mem, g_all, w_all, kg_all)


def _sb_kernel(q_ref, g_ref, kd_ref, vd_ref, kp_ref, vp_ref, o_ref, *, n_streams, q_rows, shared_past, past_rows):
    bq = HEAD_DIM
    step = pl.program_id(1)
    heads = [slice(h * HEAD_DIM, (h + 1) * HEAD_DIM) for h in range(N_HEADS)]
    chains = [(s, h) for s in range(n_streams) for h in range(N_HEADS)]
    if shared_past:
        assert q_rows == bq
        n_past = [step * n_streams + s for s in range(n_streams)]
        row0 = [s * bq for s in range(n_streams)]
    else:
        n_past = [past_rows // bq] * n_streams
        row0 = [pl.multiple_of((step * n_streams + s) * q_rows, q_rows) for s in range(n_streams)]

    def new_rows(ref, s, h):
        x = ref[pl.ds(row0[s], q_rows), heads[h]]
        if q_rows < bq:
            x = jnp.concatenate([x, jnp.zeros((bq - q_rows, HEAD_DIM), x.dtype)], axis=0)
        return x.astype(BF16)

    def past_rows_of(ref, s, h, kb):
        if shared_past:
            return ref[pl.ds(pl.multiple_of(kb * bq, bq), bq), heads[h]].astype(BF16)
        first = (s * past_rows + kb * bq) * N_HEADS + h
        return ref[pl.ds(first, bq, stride=N_HEADS), :].astype(BF16)

    q = [new_rows(q_ref, s, h) for s, h in chains]

    jj = lax.broadcasted_iota(jnp.int32, (2 * bq, 2 * bq), 0) & (bq - 1)
    ss = lax.broadcasted_iota(jnp.int32, (2 * bq, 2 * bq), 1)
    tri = jnp.where(ss >= bq, 1.0, jnp.where(jj > ss, 1.0, 0.0)).astype(BF16)
    row = lax.broadcasted_iota(jnp.int32, (bq, bq), 0)
    col = lax.broadcasted_iota(jnp.int32, (bq, bq), 1)
    causal = col < row

    def block(k_blks, v_blks, carry, acc, diag):
        zs, splits = [], []
        for c in range(len(chains)):
            z = lax.dot_general(q[c], k_blks[c], (((1,), (1,)), ((), ())), preferred_element_type=F32) * SCALE
            log_keep = -(jnp.maximum(z, 0.0) + jnp.log(1.0 + jnp.exp(-jnp.abs(z))))
            if diag:
                log_keep = jnp.where(causal, log_keep, 0.0)
            hi = log_keep.astype(BF16)
            lo = (log_keep - hi.astype(F32)).astype(BF16)
            zs.append(z + log_keep)
            splits.append(jnp.concatenate([hi, lo], axis=1))
        cum = jnp.dot(jnp.concatenate(splits, axis=0), tri, preferred_element_type=F32)
        new_carry, new_acc = [], []
        for c in range(len(chains)):
            cum_c = cum[c * bq:(c + 1) * bq]
            w = jnp.exp(zs[c] + carry[c] + cum_c[:, :bq])
            if diag:
                w = jnp.where(causal, w, 0.0)
            new_acc.append(acc[c] + jnp.dot(w.astype(BF16), v_blks[c], preferred_element_type=F32))
            new_carry.append(carry[c] + cum_c[:, bq:])
        return new_carry, new_acc

    def retire(carry, n_left):
        return tuple(jnp.where(n_left[s] > 0, carry[c], SB_DONE) for c, (s, h) in enumerate(chains))

    def top_of(carry):
        top = carry[0]
        for c in carry[1:]:
            top = jnp.maximum(top, c)
        return jnp.max(top)

    zeros = [jnp.zeros((bq, HEAD_DIM), F32)] * len(chains)
    carry, acc = block([new_rows(kd_ref, s, h) for s, h in chains], [new_rows(vd_ref, s, h) for s, h in chains],
                       zeros, zeros, True)
    carry = retire(carry, n_past)
    longest = n_past[0]
    for n in n_past[1:]:
        longest = jnp.maximum(longest, n)

    def cond(state):
        t, _, _, top = state
        return jnp.logical_and(t < longest, top > SB_EXIT)

    def body(state):
        t, carry, acc, _ = state
        kbs = [jnp.maximum(n_past[s] - 1 - t, 0) for s in range(n_streams)]
        carry, acc = block([past_rows_of(kp_ref, s, h, kbs[s]) for s, h in chains],
                           [past_rows_of(vp_ref, s, h, kbs[s]) for s, h in chains], carry, acc, False)
        carry = retire(carry, [n - 1 - t for n in n_past])
        return t + 1, carry, tuple(acc), top_of(carry)

    _, _, acc, _ = lax.while_loop(cond, body, (jnp.int32(0), carry, tuple(acc), top_of(carry)))
    for c, (s, h) in enumerate(chains):
        rows = pl.ds(row0[s], q_rows)
        o_ref[rows, heads[h]] = (acc[c][:q_rows] * _silu(g_ref[rows, heads[h]])).astype(o_ref.dtype)


def _sb_attn(q_arr, q_u, g_arr, g_u, kd_arr, kd_u, vd_arr, vd_u, past_k, past_v, past_blk0, *, n_groups, n_steps,
             n_streams, q_rows, past_rows, shared_past):
    if shared_past:
        rows = n_streams * HEAD_DIM
        blk = lambda u: pl.BlockSpec((rows, BRANCH_W), lambda g, i, u=u: (g * n_steps + i, u))
        past = pl.BlockSpec((past_rows, BRANCH_W), lambda g, i: (past_blk0 + g, 0), pipeline_mode=pl.Buffered(1))
        past_bytes = 2 * past_rows * BRANCH_W * past_k.dtype.itemsize
        out_rows = n_groups * n_steps * rows
        out_spec = pl.BlockSpec((rows, BRANCH_W), lambda g, i: (g * n_steps + i, 0))
    else:
        assert n_groups == 1 and past_blk0 % n_streams == 0
        rows = n_steps * n_streams * q_rows
        blk = lambda u: pl.BlockSpec((rows, BRANCH_W), lambda g, i, u=u: (0, u))
        past = pl.BlockSpec((n_streams * past_rows * N_HEADS, HEAD_DIM),
                            lambda g, i: (past_blk0 // n_streams + i, 0))
        past_bytes = 4 * n_streams * past_rows * BRANCH_W * past_k.dtype.itemsize
        out_rows = rows
        out_spec = pl.BlockSpec((rows, BRANCH_W), lambda g, i: (0, 0))
    est = past_bytes + 16 * rows * BRANCH_W * 4 + (4 << 20)
    return pl.pallas_call(
        functools.partial(_sb_kernel, n_streams=n_streams, q_rows=q_rows, shared_past=shared_past,
                          past_rows=past_rows),
        grid=(n_groups, n_steps),
        in_specs=[blk(q_u), blk(g_u), blk(kd_u), blk(vd_u), past, past],
        out_specs=out_spec,
        out_shape=jax.ShapeDtypeStruct((out_rows, BRANCH_W), BF16),
        compiler_params=pltpu.CompilerParams(
            dimension_semantics=("arbitrary", "arbitrary"), vmem_limit_bytes=_vmem_limit(est)),
        name="sb_attn",
    )(q_arr, g_arr, kd_arr, vd_arr, past_k, past_v)


def _branch_kernel(au_ref, av_ref, ag_ref, ca_ref, cb_ref, cg_ref, mq_ref, mg_ref, left_ref, mk_ref, mv_ref,
                   lng_ref, lnb_ref, wm_ref, pb_ref, cw_ref, cbias_ref, clng_ref, clnb_ref, qng_ref,
                   ya_ref, yc_ref, ym_ref, vn_ref, nconv_ref, vn_scr, xp_scr, sh_scr, *, nb, tb, conv_chunk):
    rows = nb * tb

    vn = _layer_norm(av_ref[...], lng_ref[...], lnb_ref[...])
    vn_ref[...] = vn
    vn_scr[...] = vn.astype(BF16)
    for c in range(rows // GMLP_CHUNK):
        rs = slice(c * GMLP_CHUNK, (c + 1) * GMLP_CHUNK)
        for g in range(N_HEADS):
            cs = slice(g * HEAD_DIM, (g + 1) * HEAD_DIM)
            s = jnp.dot(wm_ref[g], vn_scr[rs, cs], preferred_element_type=F32) + pb_ref[g]
            ya_ref[rs, cs] = (au_ref[rs, cs] * s * _silu(ag_ref[rs, cs])).astype(ya_ref.dtype)

    first_tap = HALO - (CONV_W - 1)
    sh_rows = sh_scr.shape[1]
    first = pl.program_id(1) == 0
    for b in range(nb):
        seg = slice(b * tb, (b + 1) * tb)

        @pl.when(first)
        def _():
            xp_scr[b, 0:HALO, :] = left_ref[b]

        xp_scr[b, HALO:HALO + tb, :] = ca_ref[seg, :] * _sigmoid(cb_ref[seg, :])
        for r in range(1, SUBLANES):
            sh_scr[r - 1] = xp_scr[b, r:r + sh_rows, :]
        for c in range(tb // conv_chunk):
            r0 = c * conv_chunk
            acc = jnp.broadcast_to(cbias_ref[...], (conv_chunk, BRANCH_W))
            for w in range(CONV_W):
                shift, base = (first_tap + w) % SUBLANES, (first_tap + w) // SUBLANES * SUBLANES
                if shift == 0:
                    taps = xp_scr[b, r0 + base:r0 + base + conv_chunk, :]
                else:
                    taps = sh_scr[shift - 1, r0 + base:r0 + base + conv_chunk, :]
                acc = acc + taps * cw_ref[w:w + 1, :]
            cn = _layer_norm(acc, clng_ref[...], clnb_ref[...])
            rs = slice(b * tb + r0, b * tb + r0 + conv_chunk)
            yc_ref[rs, :] = (_silu(cn) * _silu(cg_ref[rs, :])).astype(yc_ref.dtype)
        tail = xp_scr[b, tb:tb + HALO, :]
        nconv_ref[b] = tail
        xp_scr[b, 0:HALO, :] = tail

    for b in range(nb):
        seg = slice(b * tb, (b + 1) * tb)
        for h in range(N_HEADS):
            cs = slice(h * HEAD_DIM, (h + 1) * HEAD_DIM)
            qn = _rms_norm(mq_ref[seg, cs], qng_ref[...]).astype(BF16)
            s = lax.dot_general(qn, mk_ref[b, :, cs].astype(BF16), (((1,), (1,)), ((), ())),
                                preferred_element_type=F32) * SCALE
            e = jnp.exp(s - jnp.max(s, axis=-1, keepdims=True))
            p = e / jnp.sum(e, axis=-1, keepdims=True)
            o = jnp.dot(p.astype(BF16), mv_ref[b, :, cs].astype(BF16), preferred_element_type=F32)
            ym_ref[seg, cs] = (o * _silu(mg_ref[seg, cs])).astype(ym_ref.dtype)


def _branches(proj, left32, mk_arr, mv_arr, seg_blk0, p, l, *, nb, tb, n_seg_tiles, n_row_tiles):
    rows = nb * tb
    total = proj.shape[0]
    assert total == n_seg_tiles * n_row_tiles * rows and (nb == 1 or n_row_tiles == 1)
    assert rows % GMLP_CHUNK == 0 and tb % SUBLANES == 0
    n_mem = mk_arr.shape[1]
    conv_chunk = min(tb, 32)
    sh_rows = tb + HALO - SUBLANES
    col = lambda u: pl.BlockSpec((rows, BRANCH_W), lambda s, i, u=u: (s * n_row_tiles + i, u))
    mem = pl.BlockSpec((nb, n_mem, BRANCH_W), lambda s, i: (seg_blk0 + s, 0, 0))
    names = ("gmlp_ln_g", "gmlp_ln_b", "mix_w", "mix_b", "conv_w", "conv_b", "conv_ln_g", "conv_ln_b", "q_norm_g")
    in_specs = [col(U_AU), col(U_AV), col(U_AG), col(U_CA), col(U_CB), col(U_CG), col(U_MQ), col(U_MG),
                pl.BlockSpec((nb, HALO, BRANCH_W), lambda s, i: (seg_blk0 + s, 0, 0)), mem, mem]
    in_specs += [_layer_spec(p[k], l) for k in names]
    row_out = pl.BlockSpec((rows, BRANCH_W), lambda s, i: (s * n_row_tiles + i, 0))
    out_specs = [row_out, row_out, row_out, row_out,
                 pl.BlockSpec((nb, HALO, BRANCH_W), lambda s, i: (s, 0, 0))]
    out_shape = [jax.ShapeDtypeStruct((total, BRANCH_W), BF16)] * 3 + [
        jax.ShapeDtypeStruct((total, BRANCH_W), F32),
        jax.ShapeDtypeStruct((n_seg_tiles * nb, HALO, BRANCH_W), F32)]
    est = (2 * (8 * rows * BRANCH_W * 4 + 2 * nb * n_mem * BRANCH_W * 4 + 4 * rows * BRANCH_W * 4)
           + SUBLANES * (tb + HALO) * BRANCH_W * 4 + (4 << 20))
    return pl.pallas_call(
        functools.partial(_branch_kernel, nb=nb, tb=tb, conv_chunk=conv_chunk),
        grid=(n_seg_tiles, n_row_tiles),
        in_specs=in_specs,
        out_specs=out_specs,
        out_shape=out_shape,
        scratch_shapes=[pltpu.VMEM((rows, BRANCH_W), BF16), pltpu.VMEM((nb, HALO + tb, BRANCH_W), F32),
                        pltpu.VMEM((SUBLANES - 1, sh_rows, BRANCH_W), F32)],
        compiler_params=pltpu.CompilerParams(
            dimension_semantics=("arbitrary", "arbitrary"), vmem_limit_bytes=_vmem_limit(est)),
        name="branches",
    )(proj, proj, proj, proj, proj, proj, proj, proj, left32, mk_arr, mv_arr, *[p[k] for k in names])


def _merge_kernel(ya_ref, yb_ref, yc_ref, ym_ref, gl0_ref, gl1_ref, gl2_ref, gl3_ref, bg_ref, wb_ref, wo_ref, x_ref,
                  o_ref, mix_scr, *, col_chunk):
    ys = (ya_ref, yb_ref, yc_ref, ym_ref)
    gls = (gl0_ref, gl1_ref, gl2_ref, gl3_ref)
    d = x_ref.shape[1]
    for c in range(d // col_chunk):
        cs = slice(c * col_chunk, (c + 1) * col_chunk)
        mixed = None
        for n in range(N_BRANCH):
            t = jnp.dot(ys[n][...], wb_ref[n, :, cs], preferred_element_type=F32)
            t = _sigmoid(gls[n][:, cs].astype(F32) + bg_ref[n:n + 1, cs]) * t
            mixed = t if mixed is None else mixed + t
        mix_scr[:, cs] = mixed.astype(BF16)
    o_ref[...] = x_ref[...] + jnp.dot(mix_scr[...], wo_ref[...], preferred_element_type=F32)


def _merge(ya, yb, yc, ym, gates, bg_all, wb_all, wo_all, l, x, *, tm):
    rows, d = x.shape
    tm = min(tm, rows)
    assert rows % tm == 0 and gates.shape == (rows, N_BRANCH * d)
    ysp = pl.BlockSpec((tm, BRANCH_W), lambda i: (i, 0))
    gl = lambda n: pl.BlockSpec((tm, d), lambda i, n=n: (i, n))
    est = (N_BRANCH * BRANCH_W * d * 2 + d * d * 2 + 2 * (4 * tm * BRANCH_W * 2 + 4 * tm * d * 2 + 2 * tm * d * 4)
           + tm * d * 2 + 4 * tm * d * 4)
    return pl.pallas_call(
        functools.partial(_merge_kernel, col_chunk=BRANCH_W),
        grid=(rows // tm,),
        in_specs=[ysp, ysp, ysp, ysp, gl(0), gl(1), gl(2), gl(3),
                  _layer_spec(bg_all, l, pipeline_mode=pl.Buffered(1)),
                  _layer_spec(wb_all, l, pipeline_mode=pl.Buffered(1)),
                  _layer_spec(wo_all, l, pipeline_mode=pl.Buffered(1)),
                  pl.BlockSpec((tm, d), lambda i: (i, 0))],
        out_specs=pl.BlockSpec((tm, d), lambda i: (i, 0)),
        out_shape=jax.ShapeDtypeStruct((rows, d), F32),
        scratch_shapes=[pltpu.VMEM((tm, d), BF16)],
        compiler_params=pltpu.CompilerParams(
            dimension_semantics=("arbitrary",), vmem_limit_bytes=_vmem_limit(est)),
        name="merge",
    )(ya, yb, yc, ym, gates, gates, gates, gates, bg_all, wb_all, wo_all, x)


def _gmlp_mask():
    i = jnp.arange(GMLP_CHUNK)
    return (i[None, :] // CHUNK) <= (i[:, None] // CHUNK)


def kernel(x_prompt, x_sample, mem_prompt, cache_sb_k, cache_sb_v, state_conv, cache_mem_k, cache_mem_v, norm_g, w_in, gmlp_ln_g, gmlp_ln_b, gmlp_ws, gmlp_bs, conv_w, conv_b, conv_ln_g, conv_ln_b, mem_norm_g, w_mem_kv, q_norm_g, k_norm_g, b_gate, w_branch, w_out):
    n_b, seq, d = x_prompt.shape
    dec_b, dec_t, _ = x_sample.shape
    depth = w_in.shape[0]
    n_mem = mem_prompt.shape[1]
    past_len = cache_sb_k.shape[2]
    assert seq % 256 == 0 and past_len % HEAD_DIM == 0 and dec_b * dec_t == GMLP_CHUNK and dec_t <= HALO
    assert GMLP_CHUNK % dec_t == 0 and d == N_BRANCH * BRANCH_W and dec_t % SUBLANES == 0
    assert seq % (SB_STREAMS_SHARED * HEAD_DIM) == 0 and dec_b % SB_STREAMS_OWN == 0

    w_in_bf = w_in.astype(BF16)
    w_mem_bf = w_mem_kv.astype(BF16)
    wb_bf = w_branch.astype(BF16)
    wo_bf = w_out.astype(BF16)
    row = lambda a: a.reshape(depth, 1, a.shape[-1])

    ws_m = jnp.where(_gmlp_mask()[None, None], gmlp_ws, 0.0)
    eye = jnp.eye(dec_b, dtype=F32)
    corner = ws_m[:, :, :dec_t, :dec_t]
    mix_w_s = jnp.einsum("bc,lgij->lgbicj", eye, corner).reshape(depth, N_HEADS, GMLP_CHUNK, GMLP_CHUNK)
    mix_b_s = jnp.tile(gmlp_bs[:, :, :dec_t], (1, 1, dec_b))
    lane_bcast = lambda a: jnp.broadcast_to(a[..., None], a.shape + (HEAD_DIM,))
    shared = dict(gmlp_ln_g=row(gmlp_ln_g), gmlp_ln_b=row(gmlp_ln_b), conv_w=conv_w, conv_b=row(conv_b),
                  conv_ln_g=row(conv_ln_g), conv_ln_b=row(conv_ln_b), q_norm_g=row(q_norm_g))
    params_p = dict(shared, mix_w=ws_m.astype(BF16), mix_b=lane_bcast(gmlp_bs))
    params_s = dict(shared, mix_w=mix_w_s.astype(BF16), mix_b=lane_bcast(mix_b_s))
    norm_g3, mem_norm_g3, k_norm_g3 = row(norm_g), row(mem_norm_g), row(k_norm_g)

    cache_k2 = cache_sb_k.reshape(depth * dec_b * past_len * N_HEADS, HEAD_DIM)
    cache_v2 = cache_sb_v.reshape(depth * dec_b * past_len * N_HEADS, HEAD_DIM)
    cache_mk = cache_mem_k.reshape(depth * dec_b, n_mem, BRANCH_W)
    cache_mv = cache_mem_v.reshape(depth * dec_b, n_mem, BRANCH_W)
    state32 = jnp.pad(state_conv, ((0, 0), (0, 0), (HALO - (CONV_W - 1), 0), (0, 0))).reshape(
        depth * dec_b, HALO, BRANCH_W)
    zero32 = jnp.zeros((n_b, HALO, BRANCH_W), F32)

    xp = x_prompt.reshape(n_b * seq, d)
    xs = x_sample.reshape(dec_b * dec_t, d)
    mem2 = mem_prompt.reshape(n_b * n_mem, d)
    tb_p = 256
    outs = {k: [] for k in ("kp", "vp", "cp", "mkp", "mvp", "ks", "vs", "cs", "gs")}
    for l in range(depth):
        mk, mv = _memkv(mem2, mem_norm_g3, w_mem_bf, k_norm_g3, l)
        mk = mk.reshape(n_b, n_mem, BRANCH_W)
        mv = mv.reshape(n_b, n_mem, BRANCH_W)
        proj, gates, kf, vf, kb, vb = _inproj(xp, norm_g3, w_in_bf, l, tm=1024, tn=1024)
        yb = _sb_attn(proj, U_BQ, proj, U_BG, kb, 0, vb, 0, kb, vb, 0, n_groups=n_b,
                      n_steps=seq // (SB_STREAMS_SHARED * HEAD_DIM), n_streams=SB_STREAMS_SHARED, q_rows=HEAD_DIM,
                      past_rows=seq, shared_past=True)
        ya, yc, ym, _, nconv = _branches(proj, zero32, mk, mv, 0, params_p, l,
                                         nb=1, tb=tb_p, n_seg_tiles=n_b, n_row_tiles=seq // tb_p)
        xp = _merge(ya, yb, yc, ym, gates, b_gate, wb_bf, wo_bf, l, xp, tm=256)
        outs["kp"].append(kf.reshape(n_b, seq, N_HEADS, HEAD_DIM))
        outs["vp"].append(vf.reshape(n_b, seq, N_HEADS, HEAD_DIM))
        outs["cp"].append(nconv[:, HALO - (CONV_W - 1):])
        outs["mkp"].append(mk.reshape(n_b, n_mem, N_HEADS, HEAD_DIM))
        outs["mvp"].append(mv.reshape(n_b, n_mem, N_HEADS, HEAD_DIM))

        proj_s, gates_s, kf_s, vf_s, _, _ = _inproj(xs, norm_g3, w_in_bf, l, tm=GMLP_CHUNK, tn=1024)
        yb_s = _sb_attn(proj_s, U_BQ, proj_s, U_BG, proj_s, U_BK, proj_s, U_BV, cache_k2, cache_v2, l * dec_b,
                        n_groups=1, n_steps=dec_b // SB_STREAMS_OWN, n_streams=SB_STREAMS_OWN, q_rows=dec_t,
                        past_rows=past_len, shared_past=False)
        ya_s, yc_s, ym_s, vn_s, nconv_s = _branches(proj_s, state32, cache_mk, cache_mv, l, params_s, l,
                                                    nb=dec_b, tb=dec_t, n_seg_tiles=1, n_row_tiles=1)
        xs = _merge(ya_s, yb_s, yc_s, ym_s, gates_s, b_gate, wb_bf, wo_bf, l, xs, tm=GMLP_CHUNK)
        outs["ks"].append(kf_s.reshape(dec_b, dec_t, N_HEADS, HEAD_DIM))
        outs["vs"].append(vf_s.reshape(dec_b, dec_t, N_HEADS, HEAD_DIM))
        outs["cs"].append(nconv_s[:, HALO - (CONV_W - 1):])
        outs["gs"].append(vn_s.reshape(dec_b, dec_t, BRANCH_W))

    st = lambda k: jnp.stack(outs[k])
    return (xp.reshape(n_b, seq, d), xs.reshape(dec_b, dec_t, d), st("kp"), st("vp"), st("cp"), st("mkp"), st("mvp"),
            st("ks"), st("vs"), st("cs"), st("gs"))
```

```python
import functools

import jax
import jax.numpy as jnp
from jax import lax
from jax.experimental import pallas as pl
from jax.experimental.pallas import tpu as pltpu

F32 = jnp.float32
BF16 = jnp.bfloat16

HEAD_DIM = 128
N_HEADS = 4
BRANCH_W = N_HEADS * HEAD_DIM
N_BRANCH = 4
GMLP_CHUNK = 128
CHUNK = 64
CONV_W = 31
SUBLANES = 8
HALO = 32
EPS = 1e-6
SCALE = HEAD_DIM ** -0.5
U_AU, U_AV, U_AG, U_BQ, U_BK, U_BV, U_BG, U_CA, U_CB, U_CG, U_MQ, U_MG, U_GATES = (
    0, 1, 2, 3, 4, 5, 6, 7, 8, 9, 10, 11, 12)
SB_EXIT = -110.0
SB_DONE = -1e30
SB_STREAMS_SHARED = 4
SB_STREAMS_OWN = 2
V7X_VMEM_BYTES = 64 * 1024 * 1024


def _vmem_limit(estimate_bytes):
    return int(min(max(estimate_bytes * 5 // 4 + (4 << 20), 16 << 20), V7X_VMEM_BYTES - (6 << 20)))


def _layer_spec(arr, l, **kw):
    shape = arr.shape[1:]
    return pl.BlockSpec((None,) + shape, lambda *_: (l,) + (0,) * len(shape), **kw)


def _sigmoid(x):
    return 1.0 / (1.0 + jnp.exp(-x))


def _silu(x):
    return x * _sigmoid(x)


def _layer_norm(x, g, b):
    mu = jnp.mean(x, axis=-1, keepdims=True)
    xc = x - mu
    return xc * lax.rsqrt(jnp.mean(xc * xc, axis=-1, keepdims=True) + EPS) * g + b


def _rms_norm(x, g):
    return x * lax.rsqrt(jnp.mean(x * x, axis=-1, keepdims=True) + EPS) * g


def _norm_rows(x_ref, g_ref, h_scr, row_chunk):
    def body(c, carry):
        r0 = pl.multiple_of(c * row_chunk, row_chunk)
        h_scr[pl.ds(r0, row_chunk), :] = _rms_norm(x_ref[pl.ds(r0, row_chunk), :], g_ref[...]).astype(BF16)
        return carry
    lax.fori_loop(0, x_ref.shape[0] // row_chunk, body, 0)


def _inproj_kernel(x_ref, g_ref, w_ref, *rest, kv_tile, row_chunk):
    o_ref, kf_ref, vf_ref, kb_ref, vb_ref, h_scr = rest[-6:]
    j = pl.program_id(1)
    tm = x_ref.shape[0]

    @pl.when(j == 0)
    def _():
        _norm_rows(x_ref, g_ref, h_scr, row_chunk)

    acc = jnp.dot(h_scr[...], w_ref[...], preferred_element_type=F32)
    o_ref[...] = acc

    @pl.when(j == kv_tile)
    def _():
        kb_ref[...] = acc[:, :BRANCH_W].astype(BF16)
        vb_ref[...] = acc[:, BRANCH_W:].astype(BF16)
        for h in range(N_HEADS):
            kf_ref[pl.ds(h, tm, stride=N_HEADS), :] = acc[:, h * HEAD_DIM:(h + 1) * HEAD_DIM]
            vf_ref[pl.ds(h, tm, stride=N_HEADS), :] = acc[:, BRANCH_W + h * HEAD_DIM:BRANCH_W + (h + 1) * HEAD_DIM]


def _inproj(x, g_all, w_all, l, kv_prev, *, tm, tn):
    rows, d = x.shape
    depth, _, n = w_all.shape
    tm = min(tm, rows)
    assert rows % tm == 0 and n % tn == 0 and tn == 2 * BRANCH_W and (U_BK * BRANCH_W) % tn == 0 and U_BV == U_BK + 1
    kv_tile = U_BK * BRANCH_W // tn
    n_i = rows // tm
    side = pl.BlockSpec((tm, BRANCH_W), lambda i, j: (i, 0))
    heads = pl.BlockSpec((tm * N_HEADS, HEAD_DIM), lambda i, j: (l * n_i + i, 0))
    cache_rows = jax.ShapeDtypeStruct((depth * rows * N_HEADS, HEAD_DIM), F32)
    est = 2 * tm * d * 4 + tm * d * 2 + 2 * d * tn * 2 + 3 * tm * tn * 4 + 2 * tm * BRANCH_W * 12
    in_specs = [pl.BlockSpec((tm, d), lambda i, j: (i, 0)),
                _layer_spec(g_all, l),
                pl.BlockSpec((None, d, tn), lambda i, j: (l, 0, j))]
    in_specs += [pl.BlockSpec(memory_space=pl.ANY)] * 2
    assert all(a.shape == cache_rows.shape and a.dtype == cache_rows.dtype for a in kv_prev)
    return pl.pallas_call(
        functools.partial(_inproj_kernel, kv_tile=kv_tile, row_chunk=min(tm, 32)),
        grid=(n_i, n // tn),
        in_specs=in_specs,
        out_specs=[pl.BlockSpec((tm, tn), lambda i, j: (i, j)), heads, heads, side, side],
        out_shape=[jax.ShapeDtypeStruct((rows, n), F32), cache_rows, cache_rows,
                   jax.ShapeDtypeStruct((rows, BRANCH_W), BF16), jax.ShapeDtypeStruct((rows, BRANCH_W), BF16)],
        scratch_shapes=[pltpu.VMEM((tm, d), BF16)],
        input_output_aliases={3: 1, 4: 2},
        compiler_params=pltpu.CompilerParams(
            dimension_semantics=("arbitrary", "arbitrary"), vmem_limit_bytes=_vmem_limit(est)),
        name="inproj",
    )(x, g_all, w_all, *kv_prev)


def _memkv_kernel(x_ref, g_ref, w_ref, kg_ref, mk_ref, mv_ref, h_scr, *, row_chunk):
    j = pl.program_id(0)

    @pl.when(j == 0)
    def _():
        _norm_rows(x_ref, g_ref, h_scr, row_chunk)

    acc = jnp.dot(h_scr[...], w_ref[...], preferred_element_type=F32)

    @pl.when(j == 0)
    def _():
        for hh in range(N_HEADS):
            cs = slice(hh * HEAD_DIM, (hh + 1) * HEAD_DIM)
            mk_ref[:, cs] = _rms_norm(acc[:, cs], kg_ref[...])

    @pl.when(j == 1)
    def _():
        mv_ref[...] = acc


def _memkv(mem, g_all, w_all, kg_all, l):
    rows, d = mem.shape
    assert w_all.shape[2] == 2 * BRANCH_W
    whole = lambda shape: pl.BlockSpec(shape, lambda j: (0,) * len(shape))
    est = 2 * rows * d * 4 + rows * d * 2 + 2 * d * BRANCH_W * 2 + 6 * rows * BRANCH_W * 4
    return pl.pallas_call(
        functools.partial(_memkv_kernel, row_chunk=min(rows, 32)),
        grid=(2,),
        in_specs=[whole((rows, d)), _layer_spec(g_all, l), pl.BlockSpec((None, d, BRANCH_W), lambda j: (l, 0, j)),
                  _layer_spec(kg_all, l)],
        out_specs=[whole((rows, BRANCH_W)), whole((rows, BRANCH_W))],
        out_shape=[jax.ShapeDtypeStruct((rows, BRANCH_W), F32)] * 2,
        scratch_shapes=[pltpu.VMEM((rows, d), BF16)],
        compiler_params=pltpu.CompilerParams(
            dimension_semantics=("arbitrary",), vmem_limit_bytes=_vmem_limit(est)),
        name="memkv",
    )(mem, g_all, w_all, kg_all)


def _sb_kernel(q_ref, g_ref, kd_ref, vd_ref, kp_ref, vp_ref, o_ref, *, n_streams, q_rows, shared_past, past_rows):
    bq = HEAD_DIM
    step = pl.program_id(1)
    heads = [slice(h * HEAD_DIM, (h + 1) * HEAD_DIM) for h in range(N_HEADS)]
    chains = [(s, h) for s in range(n_streams) for h in range(N_HEADS)]
    if shared_past:
        assert q_rows == bq
        n_past = [step * n_streams + s for s in range(n_streams)]
        row0 = [s * bq for s in range(n_streams)]
    else:
        n_past = [past_rows // bq] * n_streams
        row0 = [pl.multiple_of((step * n_streams + s) * q_rows, q_rows) for s in range(n_streams)]

    def new_rows(ref, s, h):
        x = ref[pl.ds(row0[s], q_rows), heads[h]]
        if q_rows < bq:
            x = jnp.concatenate([x, jnp.zeros((bq - q_rows, HEAD_DIM), x.dtype)], axis=0)
        return x.astype(BF16)

    def past_rows_of(ref, s, h, kb):
        if shared_past:
            return ref[pl.ds(pl.multiple_of(kb * bq, bq), bq), heads[h]].astype(BF16)
        first = (s * past_rows + kb * bq) * N_HEADS + h
        return ref[pl.ds(first, bq, stride=N_HEADS), :].astype(BF16)

    q = [new_rows(q_ref, s, h) for s, h in chains]

    jj = lax.broadcasted_iota(jnp.int32, (2 * bq, 2 * bq), 0) & (bq - 1)
    ss = lax.broadcasted_iota(jnp.int32, (2 * bq, 2 * bq), 1)
    tri = jnp.where(ss >= bq, 1.0, jnp.where(jj > ss, 1.0, 0.0)).astype(BF16)
    row = lax.broadcasted_iota(jnp.int32, (bq, bq), 0)
    col = lax.broadcasted_iota(jnp.int32, (bq, bq), 1)
    causal = col < row

    def block(k_blks, v_blks, carry, acc, diag):
        zs, splits = [], []
        for c in range(len(chains)):
            z = lax.dot_general(q[c], k_blks[c], (((1,), (1,)), ((), ())), preferred_element_type=F32) * SCALE
            log_keep = -(jnp.maximum(z, 0.0) + jnp.log(1.0 + jnp.exp(-jnp.abs(z))))
            if diag:
                log_keep = jnp.where(causal, log_keep, 0.0)
            hi = log_keep.astype(BF16)
            lo = (log_keep - hi.astype(F32)).astype(BF16)
            zs.append(z + log_keep)
            splits.append(jnp.concatenate([hi, lo], axis=1))
        cum = jnp.dot(jnp.concatenate(splits, axis=0), tri, preferred_element_type=F32)
        new_carry, new_acc = [], []
        for c in range(len(chains)):
            cum_c = cum[c * bq:(c + 1) * bq]
            w = jnp.exp(zs[c] + carry[c] + cum_c[:, :bq])
            if diag:
                w = jnp.where(causal, w, 0.0)
            new_acc.append(acc[c] + jnp.dot(w.astype(BF16), v_blks[c], preferred_element_type=F32))
            new_carry.append(carry[c] + cum_c[:, bq:])
        return new_carry, new_acc

    def retire(carry, n_left):
        return tuple(jnp.where(n_left[s] > 0, carry[c], SB_DONE) for c, (s, h) in enumerate(chains))

    def top_of(carry):
        top = carry[0]
        for c in carry[1:]:
            top = jnp.maximum(top, c)
        return jnp.max(top)

    zeros = [jnp.zeros((bq, HEAD_DIM), F32)] * len(chains)
    carry, acc = block([new_rows(kd_ref, s, h) for s, h in chains], [new_rows(vd_ref, s, h) for s, h in chains],
                       zeros, zeros, True)
    carry = retire(carry, n_past)
    longest = n_past[0]
    for n in n_past[1:]:
        longest = jnp.maximum(longest, n)

    def cond(state):
        t, _, _, top = state
        return jnp.logical_and(t < longest, top > SB_EXIT)

    def body(state):
        t, carry, acc, _ = state
        kbs = [jnp.maximum(n_past[s] - 1 - t, 0) for s in range(n_streams)]
        carry, acc = block([past_rows_of(kp_ref, s, h, kbs[s]) for s, h in chains],
                           [past_rows_of(vp_ref, s, h, kbs[s]) for s, h in chains], carry, acc, False)
        carry = retire(carry, [n - 1 - t for n in n_past])
        return t + 1, carry, tuple(acc), top_of(carry)

    _, _, acc, _ = lax.while_loop(cond, body, (jnp.int32(0), carry, tuple(acc), top_of(carry)))
    for c, (s, h) in enumerate(chains):
        rows = pl.ds(row0[s], q_rows)
        o_ref[rows, heads[h]] = (acc[c][:q_rows] * _silu(g_ref[rows, heads[h]])).astype(o_ref.dtype)


def _sb_attn(q_arr, q_u, g_arr, g_u, kd_arr, kd_u, vd_arr, vd_u, past_k, past_v, past_blk0, *, n_groups, n_steps,
             n_streams, q_rows, past_rows, shared_past):
    if shared_past:
        rows = n_streams * HEAD_DIM
        blk = lambda u: pl.BlockSpec((rows, BRANCH_W), lambda g, i, u=u: (g * n_steps + i, u))
        past = pl.BlockSpec((past_rows, BRANCH_W), lambda g, i: (past_blk0 + g, 0), pipeline_mode=pl.Buffered(1))
        past_bytes = 2 * past_rows * BRANCH_W * past_k.dtype.itemsize
        out_rows = n_groups * n_steps * rows
        out_spec = pl.BlockSpec((rows, BRANCH_W), lambda g, i: (g * n_steps + i, 0))
    else:
        assert n_groups == 1 and past_blk0 % n_streams == 0
        rows = n_steps * n_streams * q_rows
        blk = lambda u: pl.BlockSpec((rows, BRANCH_W), lambda g, i, u=u: (0, u))
        past = pl.BlockSpec((n_streams * past_rows * N_HEADS, HEAD_DIM),
                            lambda g, i: (past_blk0 // n_streams + i, 0))
        past_bytes = 4 * n_streams * past_rows * BRANCH_W * past_k.dtype.itemsize
        out_rows = rows
        out_spec = pl.BlockSpec((rows, BRANCH_W), lambda g, i: (0, 0))
    est = past_bytes + 16 * rows * BRANCH_W * 4 + (4 << 20)
    return pl.pallas_call(
        functools.partial(_sb_kernel, n_streams=n_streams, q_rows=q_rows, shared_past=shared_past,
                          past_rows=past_rows),
        grid=(n_groups, n_steps),
        in_specs=[blk(q_u), blk(g_u), blk(kd_u), blk(vd_u), past, past],
        out_specs=out_spec,
        out_shape=jax.ShapeDtypeStruct((out_rows, BRANCH_W), BF16),
        compiler_params=pltpu.CompilerParams(
            dimension_semantics=("arbitrary", "arbitrary"), vmem_limit_bytes=_vmem_limit(est)),
        name="sb_attn",
    )(q_arr, g_arr, kd_arr, vd_arr, past_k, past_v)


def _branch_kernel(au_ref, av_ref, ag_ref, ca_ref, cb_ref, cg_ref, mq_ref, mg_ref, left_ref, mk_ref, mv_ref,
                   lng_ref, lnb_ref, wm_ref, pb_ref, cw_ref, cbias_ref, clng_ref, clnb_ref, qng_ref,
                   ya_ref, yc_ref, ym_ref, vn_ref, nconv_ref, vn_scr, xp_scr, sh_scr, *, nb, tb, conv_chunk):
    rows = nb * tb

    vn = _layer_norm(av_ref[...], lng_ref[...], lnb_ref[...])
    vn_ref[...] = vn
    vn_scr[...] = vn.astype(BF16)
    for c in range(rows // GMLP_CHUNK):
        rs = slice(c * GMLP_CHUNK, (c + 1) * GMLP_CHUNK)
        for g in range(N_HEADS):
            cs = slice(g * HEAD_DIM, (g + 1) * HEAD_DIM)
            s = jnp.dot(wm_ref[g], vn_scr[rs, cs], preferred_element_type=F32) + pb_ref[g]
            ya_ref[rs, cs] = (au_ref[rs, cs] * s * _silu(ag_ref[rs, cs])).astype(ya_ref.dtype)

    first_tap = HALO - (CONV_W - 1)
    sh_rows = sh_scr.shape[1]
    first = pl.program_id(1) == 0
    for b in range(nb):
        seg = slice(b * tb, (b + 1) * tb)

        @pl.when(first)
        def _():
            xp_scr[b, 0:HALO, :] = left_ref[b]

        xp_scr[b, HALO:HALO + tb, :] = ca_ref[seg, :] * _sigmoid(cb_ref[seg, :])
        for r in range(1, SUBLANES):
            sh_scr[r - 1] = xp_scr[b, r:r + sh_rows, :]
        for c in range(tb // conv_chunk):
            r0 = c * conv_chunk
            acc = jnp.broadcast_to(cbias_ref[...], (conv_chunk, BRANCH_W))
            for w in range(CONV_W):
                shift, base = (first_tap + w) % SUBLANES, (first_tap + w) // SUBLANES * SUBLANES
                if shift == 0:
                    taps = xp_scr[b, r0 + base:r0 + base + conv_chunk, :]
                else:
                    taps = sh_scr[shift - 1, r0 + base:r0 + base + conv_chunk, :]
                acc = acc + taps * cw_ref[w:w + 1, :]
            cn = _layer_norm(acc, clng_ref[...], clnb_ref[...])
            rs = slice(b * tb + r0, b * tb + r0 + conv_chunk)
            yc_ref[rs, :] = (_silu(cn) * _silu(cg_ref[rs, :])).astype(yc_ref.dtype)
        tail = xp_scr[b, tb:tb + HALO, :]
        nconv_ref[b] = tail
        xp_scr[b, 0:HALO, :] = tail

    for b in range(nb):
        seg = slice(b * tb, (b + 1) * tb)
        for h in range(N_HEADS):
            cs = slice(h * HEAD_DIM, (h + 1) * HEAD_DIM)
            qn = _rms_norm(mq_ref[seg, cs], qng_ref[...]).astype(BF16)
            s = lax.dot_general(qn, mk_ref[b, :, cs].astype(BF16), (((1,), (1,)), ((), ())),
                                preferred_element_type=F32) * SCALE
            e = jnp.exp(s - jnp.max(s, axis=-1, keepdims=True))
            p = e / jnp.sum(e, axis=-1, keepdims=True)
            o = jnp.dot(p.astype(BF16), mv_ref[b, :, cs].astype(BF16), preferred_element_type=F32)
            ym_ref[seg, cs] = (o * _silu(mg_ref[seg, cs])).astype(ym_ref.dtype)


def _branches(proj, left32, mk_arr, mv_arr, seg_blk0, p, l, *, nb, tb, n_seg_tiles, n_row_tiles):
    rows = nb * tb
    total = proj.shape[0]
    assert total == n_seg_tiles * n_row_tiles * rows and (nb == 1 or n_row_tiles == 1)
    assert rows % GMLP_CHUNK == 0 and tb % SUBLANES == 0
    n_mem = mk_arr.shape[1]
    conv_chunk = min(tb, 32)
    sh_rows = tb + HALO - SUBLANES
    col = lambda u: pl.BlockSpec((rows, BRANCH_W), lambda s, i, u=u: (s * n_row_tiles + i, u))
    mem = pl.BlockSpec((nb, n_mem, BRANCH_W), lambda s, i: (seg_blk0 + s, 0, 0))
    names = ("gmlp_ln_g", "gmlp_ln_b", "mix_w", "mix_b", "conv_w", "conv_b", "conv_ln_g", "conv_ln_b", "q_norm_g")
    in_specs = [col(U_AU), col(U_AV), col(U_AG), col(U_CA), col(U_CB), col(U_CG), col(U_MQ), col(U_MG),
                pl.BlockSpec((nb, HALO, BRANCH_W), lambda s, i: (seg_blk0 + s, 0, 0)), mem, mem]
    in_specs += [_layer_spec(p[k], l) for k in names]
    row_out = pl.BlockSpec((rows, BRANCH_W), lambda s, i: (s * n_row_tiles + i, 0))
    out_specs = [row_out, row_out, row_out, row_out,
                 pl.BlockSpec((nb, HALO, BRANCH_W), lambda s, i: (s, 0, 0))]
    out_shape = [jax.ShapeDtypeStruct((total, BRANCH_W), BF16)] * 3 + [
        jax.ShapeDtypeStruct((total, BRANCH_W), F32),
        jax.ShapeDtypeStruct((n_seg_tiles * nb, HALO, BRANCH_W), F32)]
    est = (2 * (8 * rows * BRANCH_W * 4 + 2 * nb * n_mem * BRANCH_W * 4 + 4 * rows * BRANCH_W * 4)
           + SUBLANES * (tb + HALO) * BRANCH_W * 4 + (4 << 20))
    return pl.pallas_call(
        functools.partial(_branch_kernel, nb=nb, tb=tb, conv_chunk=conv_chunk),
        grid=(n_seg_tiles, n_row_tiles),
        in_specs=in_specs,
        out_specs=out_specs,
        out_shape=out_shape,
        scratch_shapes=[pltpu.VMEM((rows, BRANCH_W), BF16), pltpu.VMEM((nb, HALO + tb, BRANCH_W), F32),
                        pltpu.VMEM((SUBLANES - 1, sh_rows, BRANCH_W), F32)],
        compiler_params=pltpu.CompilerParams(
            dimension_semantics=("arbitrary", "arbitrary"), vmem_limit_bytes=_vmem_limit(est)),
        name="branches",
    )(proj, proj, proj, proj, proj, proj, proj, proj, left32, mk_arr, mv_arr, *[p[k] for k in names])


def _merge_kernel(ya_ref, yb_ref, yc_ref, ym_ref, gl0_ref, gl1_ref, gl2_ref, gl3_ref, bg_ref, wb_ref, wo_ref, x_ref,
                  o_ref, mix_scr, *, col_chunk):
    ys = (ya_ref, yb_ref, yc_ref, ym_ref)
    gls = (gl0_ref, gl1_ref, gl2_ref, gl3_ref)
    d = x_ref.shape[1]
    for c in range(d // col_chunk):
        cs = slice(c * col_chunk, (c + 1) * col_chunk)
        mixed = None
        for n in range(N_BRANCH):
            t = jnp.dot(ys[n][...], wb_ref[n, :, cs], preferred_element_type=F32)
            t = _sigmoid(gls[n][:, cs] + bg_ref[n:n + 1, cs]) * t
            mixed = t if mixed is None else mixed + t
        mix_scr[:, cs] = mixed.astype(BF16)
    o_ref[...] = x_ref[...] + jnp.dot(mix_scr[...], wo_ref[...], preferred_element_type=F32)


def _merge(ya, yb, yc, ym, proj, bg_all, wb_all, wo_all, l, x, *, tm):
    rows, d = x.shape
    tm = min(tm, rows)
    assert rows % tm == 0 and (U_GATES * BRANCH_W) % d == 0
    gate0 = U_GATES * BRANCH_W // d
    ysp = pl.BlockSpec((tm, BRANCH_W), lambda i: (i, 0))
    gl = lambda n: pl.BlockSpec((tm, d), lambda i, n=n: (i, gate0 + n))
    est = (N_BRANCH * BRANCH_W * d * 2 + d * d * 2 + 2 * (4 * tm * BRANCH_W * 2 + 4 * tm * d * 4 + 2 * tm * d * 4)
           + tm * d * 2 + 4 * tm * d * 4)
    return pl.pallas_call(
        functools.partial(_merge_kernel, col_chunk=BRANCH_W),
        grid=(rows // tm,),
        in_specs=[ysp, ysp, ysp, ysp, gl(0), gl(1), gl(2), gl(3),
                  _layer_spec(bg_all, l, pipeline_mode=pl.Buffered(1)),
                  _layer_spec(wb_all, l, pipeline_mode=pl.Buffered(1)),
                  _layer_spec(wo_all, l, pipeline_mode=pl.Buffered(1)),
                  pl.BlockSpec((tm, d), lambda i: (i, 0))],
        out_specs=pl.BlockSpec((tm, d), lambda i: (i, 0)),
        out_shape=jax.ShapeDtypeStruct((rows, d), F32),
        scratch_shapes=[pltpu.VMEM((tm, d), BF16)],
        compiler_params=pltpu.CompilerParams(
            dimension_semantics=("arbitrary",), vmem_limit_bytes=_vmem_limit(est)),
        name="merge",
    )(ya, yb, yc, ym, proj, proj, proj, proj, bg_all, wb_all, wo_all, x)


def _gmlp_mask():
    i = jnp.arange(GMLP_CHUNK)
    return (i[None, :] // CHUNK) <= (i[:, None] // CHUNK)


def kernel(x_prompt, x_sample, mem_prompt, cache_sb_k, cache_sb_v, state_conv, cache_mem_k, cache_mem_v, norm_g, w_in, gmlp_ln_g, gmlp_ln_b, gmlp_ws, gmlp_bs, conv_w, conv_b, conv_ln_g, conv_ln_b, mem_norm_g, w_mem_kv, q_norm_g, k_norm_g, b_gate, w_branch, w_out):
    n_b, seq, d = x_prompt.shape
    dec_b, dec_t, _ = x_sample.shape
    depth = w_in.shape[0]
    n_mem = mem_prompt.shape[1]
    past_len = cache_sb_k.shape[2]
    assert seq % 256 == 0 and past_len % HEAD_DIM == 0 and dec_b * dec_t == GMLP_CHUNK and dec_t <= HALO
    assert GMLP_CHUNK % dec_t == 0 and d == N_BRANCH * BRANCH_W and dec_t % SUBLANES == 0
    assert seq % (SB_STREAMS_SHARED * HEAD_DIM) == 0 and dec_b % SB_STREAMS_OWN == 0

    w_in_bf = w_in.astype(BF16)
    w_mem_bf = w_mem_kv.astype(BF16)
    wb_bf = w_branch.astype(BF16)
    wo_bf = w_out.astype(BF16)
    row = lambda a: a.reshape(depth, 1, a.shape[-1])

    ws_m = jnp.where(_gmlp_mask()[None, None], gmlp_ws, 0.0)
    eye = jnp.eye(dec_b, dtype=F32)
    corner = ws_m[:, :, :dec_t, :dec_t]
    mix_w_s = jnp.einsum("bc,lgij->lgbicj", eye, corner).reshape(depth, N_HEADS, GMLP_CHUNK, GMLP_CHUNK)
    mix_b_s = jnp.tile(gmlp_bs[:, :, :dec_t], (1, 1, dec_b))
    lane_bcast = lambda a: jnp.broadcast_to(a[..., None], a.shape + (HEAD_DIM,))
    shared = dict(gmlp_ln_g=row(gmlp_ln_g), gmlp_ln_b=row(gmlp_ln_b), conv_w=conv_w, conv_b=row(conv_b),
                  conv_ln_g=row(conv_ln_g), conv_ln_b=row(conv_ln_b), q_norm_g=row(q_norm_g))
    params_p = dict(shared, mix_w=ws_m.astype(BF16), mix_b=lane_bcast(gmlp_bs))
    params_s = dict(shared, mix_w=mix_w_s.astype(BF16), mix_b=lane_bcast(mix_b_s))
    norm_g3, mem_norm_g3, k_norm_g3 = row(norm_g), row(mem_norm_g), row(k_norm_g)

    cache_k2 = cache_sb_k.reshape(depth * dec_b * past_len * N_HEADS, HEAD_DIM)
    cache_v2 = cache_sb_v.reshape(depth * dec_b * past_len * N_HEADS, HEAD_DIM)
    cache_mk = cache_mem_k.reshape(depth * dec_b, n_mem, BRANCH_W)
    cache_mv = cache_mem_v.reshape(depth * dec_b, n_mem, BRANCH_W)
    state32 = jnp.pad(state_conv, ((0, 0), (0, 0), (HALO - (CONV_W - 1), 0), (0, 0))).reshape(
        depth * dec_b, HALO, BRANCH_W)
    zero32 = jnp.zeros((n_b, HALO, BRANCH_W), F32)

    xp = x_prompt.reshape(n_b * seq, d)
    xs = x_sample.reshape(dec_b * dec_t, d)
    mem2 = mem_prompt.reshape(n_b * n_mem, d)
    tb_p = 256
    outs = {k: [] for k in ("cp", "mkp", "mvp", "cs", "gs")}
    fill = lambda rows: tuple(jnp.full((depth * rows * N_HEADS, HEAD_DIM), c, F32) for c in (0.0, 1.0))
    kv_p, kv_s = fill(n_b * seq), fill(dec_b * dec_t)
    for l in range(depth):
        mk, mv = _memkv(mem2, mem_norm_g3, w_mem_bf, k_norm_g3, l)
        mk = mk.reshape(n_b, n_mem, BRANCH_W)
        mv = mv.reshape(n_b, n_mem, BRANCH_W)
        proj, kf, vf, kb, vb = _inproj(xp, norm_g3, w_in_bf, l, kv_p, tm=1024, tn=1024)
        kv_p = (kf, vf)
        yb = _sb_attn(proj, U_BQ, proj, U_BG, kb, 0, vb, 0, kb, vb, 0, n_groups=n_b,
                      n_steps=seq // (SB_STREAMS_SHARED * HEAD_DIM), n_streams=SB_STREAMS_SHARED, q_rows=HEAD_DIM,
                      past_rows=seq, shared_past=True)
        ya, yc, ym, _, nconv = _branches(proj, zero32, mk, mv, 0, params_p, l,
                                         nb=1, tb=tb_p, n_seg_tiles=n_b, n_row_tiles=seq // tb_p)
        xp = _merge(ya, yb, yc, ym, proj, b_gate, wb_bf, wo_bf, l, xp, tm=256)
        outs["cp"].append(nconv[:, HALO - (CONV_W - 1):])
        outs["mkp"].append(mk.reshape(n_b, n_mem, N_HEADS, HEAD_DIM))
        outs["mvp"].append(mv.reshape(n_b, n_mem, N_HEADS, HEAD_DIM))

        proj_s, kf_s, vf_s, _, _ = _inproj(xs, norm_g3, w_in_bf, l, kv_s, tm=GMLP_CHUNK, tn=1024)
        kv_s = (kf_s, vf_s)
        yb_s = _sb_attn(proj_s, U_BQ, proj_s, U_BG, proj_s, U_BK, proj_s, U_BV, cache_k2, cache_v2, l * dec_b,
                        n_groups=1, n_steps=dec_b // SB_STREAMS_OWN, n_streams=SB_STREAMS_OWN, q_rows=dec_t,
                        past_rows=past_len, shared_past=False)
        ya_s, yc_s, ym_s, vn_s, nconv_s = _branches(proj_s, state32, cache_mk, cache_mv, l, params_s, l,
                                                    nb=dec_b, tb=dec_t, n_seg_tiles=1, n_row_tiles=1)
        xs = _merge(ya_s, yb_s, yc_s, ym_s, proj_s, b_gate, wb_bf, wo_bf, l, xs, tm=GMLP_CHUNK)
        outs["cs"].append(nconv_s[:, HALO - (CONV_W - 1):])
        outs["gs"].append(vn_s.reshape(dec_b, dec_t, BRANCH_W))

    st = lambda k: jnp.stack(outs[k])
    heads_p = lambda a: a.reshape(depth, n_b, seq, N_HEADS, HEAD_DIM)
    heads_s = lambda a: a.reshape(depth, dec_b, dec_t, N_HEADS, HEAD_DIM)
    return (xp.reshape(n_b, seq, d), xs.reshape(dec_b, dec_t, d), heads_p(kv_p[0]), heads_p(kv_p[1]), st("cp"),
            st("mkp"), st("mvp"), heads_s(kv_s[0]), heads_s(kv_s[1]), st("cs"), st("gs"))
```

```python
import functools

import jax
import jax.numpy as jnp
from jax import lax
from jax.experimental import pallas as pl
from jax.experimental.pallas import tpu as pltpu

F32 = jnp.float32
BF16 = jnp.bfloat16

HEAD_DIM = 128
N_HEADS = 4
BRANCH_W = N_HEADS * HEAD_DIM
N_BRANCH = 4
GMLP_CHUNK = 128
CHUNK = 64
CONV_W = 31
SUBLANES = 8
HALO = 32
EPS = 1e-6
SCALE = HEAD_DIM ** -0.5
U_AU, U_AV, U_AG, U_BQ, U_BK, U_BV, U_BG, U_CA, U_CB, U_CG, U_MQ, U_MG, U_GATES = (
    0, 1, 2, 3, 4, 5, 6, 7, 8, 9, 10, 11, 12)
SB_EXIT = -110.0
SB_DONE = -1e30
SB_STREAMS_SHARED = 4
SB_STREAMS_OWN = 2
V7X_VMEM_BYTES = 64 * 1024 * 1024


def _vmem_limit(estimate_bytes):
    return int(min(max(estimate_bytes * 5 // 4 + (4 << 20), 16 << 20), V7X_VMEM_BYTES - (6 << 20)))


def _layer_spec(arr, l, **kw):
    shape = arr.shape[1:]
    return pl.BlockSpec((None,) + shape, lambda *_: (l,) + (0,) * len(shape), **kw)


def _sigmoid(x):
    return 1.0 / (1.0 + jnp.exp(-x))


def _silu(x):
    return x * _sigmoid(x)


def _layer_norm(x, g, b):
    mu = jnp.mean(x, axis=-1, keepdims=True)
    xc = x - mu
    return xc * lax.rsqrt(jnp.mean(xc * xc, axis=-1, keepdims=True) + EPS) * g + b


def _rms_norm(x, g):
    return x * lax.rsqrt(jnp.mean(x * x, axis=-1, keepdims=True) + EPS) * g


def _norm_rows(x_ref, g_ref, h_scr, row_chunk):
    def body(c, carry):
        r0 = pl.multiple_of(c * row_chunk, row_chunk)
        h_scr[pl.ds(r0, row_chunk), :] = _rms_norm(x_ref[pl.ds(r0, row_chunk), :], g_ref[...]).astype(BF16)
        return carry
    n_chunks = x_ref.shape[0] // row_chunk
    lax.fori_loop(0, n_chunks, body, 0, unroll=min(n_chunks, 4))


def _inproj_kernel(x_ref, g_ref, w_ref, k_prev, v_prev, o_ref, kf_ref, vf_ref, kb_ref, vb_ref, *scratch, kv_tile,
                   row_chunk):
    j = pl.program_id(1)
    tm = x_ref.shape[0]
    if scratch:
        h_ref, = scratch

        @pl.when(j == 0)
        def _():
            _norm_rows(x_ref, g_ref, h_ref, row_chunk)
    else:
        h_ref = x_ref

    acc = jnp.dot(h_ref[...], w_ref[...], preferred_element_type=F32)
    o_ref[...] = acc

    @pl.when(j == kv_tile)
    def _():
        kb_ref[...] = acc[:, :BRANCH_W].astype(BF16)
        vb_ref[...] = acc[:, BRANCH_W:].astype(BF16)
        for h in range(N_HEADS):
            kf_ref[pl.ds(h, tm, stride=N_HEADS), :] = acc[:, h * HEAD_DIM:(h + 1) * HEAD_DIM]
            vf_ref[pl.ds(h, tm, stride=N_HEADS), :] = acc[:, BRANCH_W + h * HEAD_DIM:BRANCH_W + (h + 1) * HEAD_DIM]


def _inproj(x, g_all, w_all, l, kv_prev, *, tm, tn):
    pre_normed = x.dtype == BF16
    rows, d = x.shape
    depth, _, n = w_all.shape
    tm = min(tm, rows)
    assert rows % tm == 0 and n % tn == 0 and tn == 2 * BRANCH_W and (U_BK * BRANCH_W) % tn == 0 and U_BV == U_BK + 1
    kv_tile = U_BK * BRANCH_W // tn
    n_i = rows // tm
    side = pl.BlockSpec((tm, BRANCH_W), lambda i, j: (i, 0))
    heads = pl.BlockSpec((tm * N_HEADS, HEAD_DIM), lambda i, j: (l * n_i + i, 0))
    cache_rows = jax.ShapeDtypeStruct((depth * rows * N_HEADS, HEAD_DIM), F32)
    est = 2 * tm * d * 4 + tm * d * 2 + 2 * d * tn * 2 + 3 * tm * tn * 4 + 2 * tm * BRANCH_W * 12
    in_specs = [pl.BlockSpec((tm, d), lambda i, j: (i, 0)),
                _layer_spec(g_all, l),
                pl.BlockSpec((None, d, tn), lambda i, j: (l, 0, j))]
    in_specs += [pl.BlockSpec(memory_space=pl.ANY)] * 2
    assert all(a.shape == cache_rows.shape and a.dtype == cache_rows.dtype for a in kv_prev)
    return pl.pallas_call(
        functools.partial(_inproj_kernel, kv_tile=kv_tile, row_chunk=min(tm, 32)),
        grid=(n_i, n // tn),
        in_specs=in_specs,
        out_specs=[pl.BlockSpec((tm, tn), lambda i, j: (i, j)), heads, heads, side, side],
        out_shape=[jax.ShapeDtypeStruct((rows, n), F32), cache_rows, cache_rows,
                   jax.ShapeDtypeStruct((rows, BRANCH_W), BF16), jax.ShapeDtypeStruct((rows, BRANCH_W), BF16)],
        scratch_shapes=[] if pre_normed else [pltpu.VMEM((tm, d), BF16)],
        input_output_aliases={3: 1, 4: 2},
        compiler_params=pltpu.CompilerParams(
            dimension_semantics=("arbitrary", "arbitrary"), vmem_limit_bytes=_vmem_limit(est)),
        name="inproj",
    )(x, g_all, w_all, *kv_prev)


def _memkv_kernel(x_ref, g_ref, w_ref, kg_ref, mk_ref, mv_ref, h_scr, *, row_chunk):
    j = pl.program_id(0)

    @pl.when(j == 0)
    def _():
        _norm_rows(x_ref, g_ref, h_scr, row_chunk)

    acc = jnp.dot(h_scr[...], w_ref[...], preferred_element_type=F32)

    @pl.when(j == 0)
    def _():
        for hh in range(N_HEADS):
            cs = slice(hh * HEAD_DIM, (hh + 1) * HEAD_DIM)
            mk_ref[:, cs] = _rms_norm(acc[:, cs], kg_ref[...])

    @pl.when(j == 1)
    def _():
        mv_ref[...] = acc


def _memkv(mem, g_all, w_all, kg_all, l):
    rows, d = mem.shape
    assert w_all.shape[2] == 2 * BRANCH_W
    whole = lambda shape: pl.BlockSpec(shape, lambda j: (0,) * len(shape))
    est = 2 * rows * d * 4 + rows * d * 2 + 2 * d * BRANCH_W * 2 + 6 * rows * BRANCH_W * 4
    return pl.pallas_call(
        functools.partial(_memkv_kernel, row_chunk=min(rows, 32)),
        grid=(2,),
        in_specs=[whole((rows, d)), _layer_spec(g_all, l), pl.BlockSpec((None, d, BRANCH_W), lambda j: (l, 0, j)),
                  _layer_spec(kg_all, l)],
        out_specs=[whole((rows, BRANCH_W)), whole((rows, BRANCH_W))],
        out_shape=[jax.ShapeDtypeStruct((rows, BRANCH_W), F32)] * 2,
        scratch_shapes=[pltpu.VMEM((rows, d), BF16)],
        compiler_params=pltpu.CompilerParams(
            dimension_semantics=("arbitrary",), vmem_limit_bytes=_vmem_limit(est)),
        name="memkv",
    )(mem, g_all, w_all, kg_all)


def _sb_kernel(q_ref, g_ref, kd_ref, vd_ref, kp_ref, vp_ref, o_ref, *, n_streams, q_rows, shared_past, past_rows):
    bq = HEAD_DIM
    step = pl.program_id(1)
    heads = [slice(h * HEAD_DIM, (h + 1) * HEAD_DIM) for h in range(N_HEADS)]
    chains = [(s, h) for s in range(n_streams) for h in range(N_HEADS)]
    if shared_past:
        assert q_rows == bq
        n_past = [step * n_streams + s for s in range(n_streams)]
        row0 = [s * bq for s in range(n_streams)]
    else:
        n_past = [past_rows // bq] * n_streams
        row0 = [pl.multiple_of((step * n_streams + s) * q_rows, q_rows) for s in range(n_streams)]

    def new_rows(ref, s, h):
        x = ref[pl.ds(row0[s], q_rows), heads[h]]
        if q_rows < bq:
            x = jnp.concatenate([x, jnp.zeros((bq - q_rows, HEAD_DIM), x.dtype)], axis=0)
        return x.astype(BF16)

    def past_rows_of(ref, s, h, kb):
        if shared_past:
            return ref[pl.ds(pl.multiple_of(kb * bq, bq), bq), heads[h]].astype(BF16)
        first = (s * past_rows + kb * bq) * N_HEADS + h
        return ref[pl.ds(first, bq, stride=N_HEADS), :].astype(BF16)

    q = [new_rows(q_ref, s, h) for s, h in chains]

    jj = lax.broadcasted_iota(jnp.int32, (2 * bq, 2 * bq), 0) & (bq - 1)
    ss = lax.broadcasted_iota(jnp.int32, (2 * bq, 2 * bq), 1)
    tri = jnp.where(ss >= bq, 1.0, jnp.where(jj > ss, 1.0, 0.0)).astype(BF16)
    row = lax.broadcasted_iota(jnp.int32, (bq, bq), 0)
    col = lax.broadcasted_iota(jnp.int32, (bq, bq), 1)
    causal = col < row

    def block(k_blks, v_blks, carry, acc, diag):
        zs, splits = [], []
        for c in range(len(chains)):
            z = lax.dot_general(q[c], k_blks[c], (((1,), (1,)), ((), ())), preferred_element_type=F32) * SCALE
            log_keep = -(jnp.maximum(z, 0.0) + jnp.log(1.0 + jnp.exp(-jnp.abs(z))))
            if diag:
                log_keep = jnp.where(causal, log_keep, 0.0)
            hi = log_keep.astype(BF16)
            lo = (log_keep - hi.astype(F32)).astype(BF16)
            zs.append(z + log_keep)
            splits.append(jnp.concatenate([hi, lo], axis=1))
        cum = jnp.dot(jnp.concatenate(splits, axis=0), tri, preferred_element_type=F32)
        new_carry, new_acc = [], []
        for c in range(len(chains)):
            cum_c = cum[c * bq:(c + 1) * bq]
            w = jnp.exp(zs[c] + carry[c] + cum_c[:, :bq])
            if diag:
                w = jnp.where(causal, w, 0.0)
            new_acc.append(acc[c] + jnp.dot(w.astype(BF16), v_blks[c], preferred_element_type=F32))
            new_carry.append(carry[c] + cum_c[:, bq:])
        return new_carry, new_acc

    def retire(carry, n_left):
        return tuple(jnp.where(n_left[s] > 0, carry[c], SB_DONE) for c, (s, h) in enumerate(chains))

    def top_of(carry):
        top = carry[0]
        for c in carry[1:]:
            top = jnp.maximum(top, c)
        return jnp.max(top)

    zeros = [jnp.zeros((bq, HEAD_DIM), F32)] * len(chains)
    carry, acc = block([new_rows(kd_ref, s, h) for s, h in chains], [new_rows(vd_ref, s, h) for s, h in chains],
                       zeros, zeros, True)
    carry = retire(carry, n_past)
    longest = n_past[0]
    for n in n_past[1:]:
        longest = jnp.maximum(longest, n)

    def cond(state):
        t, _, _, top = state
        return jnp.logical_and(t < longest, top > SB_EXIT)

    def body(state):
        t, carry, acc, _ = state
        kbs = [jnp.maximum(n_past[s] - 1 - t, 0) for s in range(n_streams)]
        carry, acc = block([past_rows_of(kp_ref, s, h, kbs[s]) for s, h in chains],
                           [past_rows_of(vp_ref, s, h, kbs[s]) for s, h in chains], carry, acc, False)
        carry = retire(carry, [n - 1 - t for n in n_past])
        return t + 1, carry, tuple(acc), top_of(carry)

    _, _, acc, _ = lax.while_loop(cond, body, (jnp.int32(0), carry, tuple(acc), top_of(carry)))
    for c, (s, h) in enumerate(chains):
        rows = pl.ds(row0[s], q_rows)
        o_ref[rows, heads[h]] = (acc[c][:q_rows] * _silu(g_ref[rows, heads[h]])).astype(o_ref.dtype)


def _sb_attn(q_arr, q_u, g_arr, g_u, kd_arr, kd_u, vd_arr, vd_u, past_k, past_v, past_blk0, *, n_groups, n_steps,
             n_streams, q_rows, past_rows, shared_past):
    if shared_past:
        rows = n_streams * HEAD_DIM
        blk = lambda u: pl.BlockSpec((rows, BRANCH_W), lambda g, i, u=u: (g * n_steps + i, u))
        past = pl.BlockSpec((past_rows, BRANCH_W), lambda g, i: (past_blk0 + g, 0), pipeline_mode=pl.Buffered(1))
        past_bytes = 2 * past_rows * BRANCH_W * past_k.dtype.itemsize
        out_rows = n_groups * n_steps * rows
        out_spec = pl.BlockSpec((rows, BRANCH_W), lambda g, i: (g * n_steps + i, 0))
    else:
        assert n_groups == 1 and past_blk0 % n_streams == 0
        rows = n_steps * n_streams * q_rows
        blk = lambda u: pl.BlockSpec((rows, BRANCH_W), lambda g, i, u=u: (0, u))
        past = pl.BlockSpec((n_streams * past_rows * N_HEADS, HEAD_DIM),
                            lambda g, i: (past_blk0 // n_streams + i, 0))
        past_bytes = 4 * n_streams * past_rows * BRANCH_W * past_k.dtype.itemsize
        out_rows = rows
        out_spec = pl.BlockSpec((rows, BRANCH_W), lambda g, i: (0, 0))
    est = past_bytes + 16 * rows * BRANCH_W * 4 + (4 << 20)
    return pl.pallas_call(
        functools.partial(_sb_kernel, n_streams=n_streams, q_rows=q_rows, shared_past=shared_past,
                          past_rows=past_rows),
        grid=(n_groups, n_steps),
        in_specs=[blk(q_u), blk(g_u), blk(kd_u), blk(vd_u), past, past],
        out_specs=out_spec,
        out_shape=jax.ShapeDtypeStruct((out_rows, BRANCH_W), BF16),
        compiler_params=pltpu.CompilerParams(
            dimension_semantics=("arbitrary", "arbitrary"), vmem_limit_bytes=_vmem_limit(est)),
        name="sb_attn",
    )(q_arr, g_arr, kd_arr, vd_arr, past_k, past_v)


def _branch_kernel(au_ref, av_ref, ag_ref, ca_ref, cb_ref, cg_ref, mq_ref, mg_ref, left_ref, mk_ref, mv_ref,
                   lng_ref, lnb_ref, wm_ref, pb_ref, cw_ref, cbias_ref, clng_ref, clnb_ref, qng_ref,
                   ya_ref, yc_ref, ym_ref, vn_ref, nconv_ref, vn_scr, xp_scr, sh_scr, *, nb, tb, conv_chunk):
    rows = nb * tb

    vn = _layer_norm(av_ref[...], lng_ref[...], lnb_ref[...])
    vn_ref[...] = vn
    vn_scr[...] = vn.astype(BF16)
    for c in range(rows // GMLP_CHUNK):
        rs = slice(c * GMLP_CHUNK, (c + 1) * GMLP_CHUNK)
        for g in range(N_HEADS):
            cs = slice(g * HEAD_DIM, (g + 1) * HEAD_DIM)
            s = jnp.dot(wm_ref[g], vn_scr[rs, cs], preferred_element_type=F32) + pb_ref[g]
            ya_ref[rs, cs] = (au_ref[rs, cs] * s * _silu(ag_ref[rs, cs])).astype(ya_ref.dtype)

    first_tap = HALO - (CONV_W - 1)
    sh_rows = sh_scr.shape[1]
    first = pl.program_id(1) == 0
    for b in range(nb):
        seg = slice(b * tb, (b + 1) * tb)

        @pl.when(first)
        def _():
            xp_scr[b, 0:HALO, :] = left_ref[b]

        xp_scr[b, HALO:HALO + tb, :] = ca_ref[seg, :] * _sigmoid(cb_ref[seg, :])
        for r in range(1, SUBLANES):
            sh_scr[r - 1] = xp_scr[b, r:r + sh_rows, :]
        for c in range(tb // conv_chunk):
            r0 = c * conv_chunk
            acc = jnp.broadcast_to(cbias_ref[...], (conv_chunk, BRANCH_W))
            for w in range(CONV_W):
                shift, base = (first_tap + w) % SUBLANES, (first_tap + w) // SUBLANES * SUBLANES
                if shift == 0:
                    taps = xp_scr[b, r0 + base:r0 + base + conv_chunk, :]
                else:
                    taps = sh_scr[shift - 1, r0 + base:r0 + base + conv_chunk, :]
                acc = acc + taps * cw_ref[w:w + 1, :]
            cn = _layer_norm(acc, clng_ref[...], clnb_ref[...])
            rs = slice(b * tb + r0, b * tb + r0 + conv_chunk)
            yc_ref[rs, :] = (_silu(cn) * _silu(cg_ref[rs, :])).astype(yc_ref.dtype)
        tail = xp_scr[b, tb:tb + HALO, :]
        nconv_ref[b] = tail
        xp_scr[b, 0:HALO, :] = tail

    for b in range(nb):
        seg = slice(b * tb, (b + 1) * tb)
        for h in range(N_HEADS):
            cs = slice(h * HEAD_DIM, (h + 1) * HEAD_DIM)
            qn = _rms_norm(mq_ref[seg, cs], qng_ref[...]).astype(BF16)
            s = lax.dot_general(qn, mk_ref[b, :, cs].astype(BF16), (((1,), (1,)), ((), ())),
                                preferred_element_type=F32) * SCALE
            e = jnp.exp(s - jnp.max(s, axis=-1, keepdims=True))
            p = e / jnp.sum(e, axis=-1, keepdims=True)
            o = jnp.dot(p.astype(BF16), mv_ref[b, :, cs].astype(BF16), preferred_element_type=F32)
            ym_ref[seg, cs] = (o * _silu(mg_ref[seg, cs])).astype(ym_ref.dtype)


def _branches(proj, left32, mk_arr, mv_arr, seg_blk0, p, l, *, nb, tb, n_seg_tiles, n_row_tiles):
    rows = nb * tb
    total = proj.shape[0]
    assert total == n_seg_tiles * n_row_tiles * rows and (nb == 1 or n_row_tiles == 1)
    assert rows % GMLP_CHUNK == 0 and tb % SUBLANES == 0
    n_mem = mk_arr.shape[1]
    conv_chunk = min(tb, 32)
    sh_rows = tb + HALO - SUBLANES
    col = lambda u: pl.BlockSpec((rows, BRANCH_W), lambda s, i, u=u: (s * n_row_tiles + i, u))
    mem = pl.BlockSpec((nb, n_mem, BRANCH_W), lambda s, i: (seg_blk0 + s, 0, 0))
    names = ("gmlp_ln_g", "gmlp_ln_b", "mix_w", "mix_b", "conv_w", "conv_b", "conv_ln_g", "conv_ln_b", "q_norm_g")
    in_specs = [col(U_AU), col(U_AV), col(U_AG), col(U_CA), col(U_CB), col(U_CG), col(U_MQ), col(U_MG),
                pl.BlockSpec((nb, HALO, BRANCH_W), lambda s, i: (seg_blk0 + s, 0, 0)), mem, mem]
    in_specs += [_layer_spec(p[k], l) for k in names]
    row_out = pl.BlockSpec((rows, BRANCH_W), lambda s, i: (s * n_row_tiles + i, 0))
    out_specs = [row_out, row_out, row_out, row_out,
                 pl.BlockSpec((nb, HALO, BRANCH_W), lambda s, i: (s, 0, 0))]
    out_shape = [jax.ShapeDtypeStruct((total, BRANCH_W), BF16)] * 3 + [
        jax.ShapeDtypeStruct((total, BRANCH_W), F32),
        jax.ShapeDtypeStruct((n_seg_tiles * nb, HALO, BRANCH_W), F32)]
    est = (2 * (8 * rows * BRANCH_W * 4 + 2 * nb * n_mem * BRANCH_W * 4 + 4 * rows * BRANCH_W * 4)
           + SUBLANES * (tb + HALO) * BRANCH_W * 4 + (4 << 20))
    return pl.pallas_call(
        functools.partial(_branch_kernel, nb=nb, tb=tb, conv_chunk=conv_chunk),
        grid=(n_seg_tiles, n_row_tiles),
        in_specs=in_specs,
        out_specs=out_specs,
        out_shape=out_shape,
        scratch_shapes=[pltpu.VMEM((rows, BRANCH_W), BF16), pltpu.VMEM((nb, HALO + tb, BRANCH_W), F32),
                        pltpu.VMEM((SUBLANES - 1, sh_rows, BRANCH_W), F32)],
        compiler_params=pltpu.CompilerParams(
            dimension_semantics=("arbitrary", "arbitrary"), vmem_limit_bytes=_vmem_limit(est)),
        name="branches",
    )(proj, proj, proj, proj, proj, proj, proj, proj, left32, mk_arr, mv_arr, *[p[k] for k in names])


def _merge_kernel(ya_ref, yb_ref, yc_ref, ym_ref, gl0_ref, gl1_ref, gl2_ref, gl3_ref, bg_ref, wb_ref, wo_ref, x_ref,
                  *rest, col_chunk, row_chunk):
    o_ref, mix_scr = rest[-3 if len(rest) == 4 else -2], rest[-1]
    ys = (ya_ref, yb_ref, yc_ref, ym_ref)
    gls = (gl0_ref, gl1_ref, gl2_ref, gl3_ref)
    d = x_ref.shape[1]
    for c in range(d // col_chunk):
        cs = slice(c * col_chunk, (c + 1) * col_chunk)
        mixed = None
        for n in range(N_BRANCH):
            t = jnp.dot(ys[n][...], wb_ref[n, :, cs], preferred_element_type=F32)
            t = _sigmoid(gls[n][:, cs] + bg_ref[n:n + 1, cs]) * t
            mixed = t if mixed is None else mixed + t
        mix_scr[:, cs] = mixed.astype(BF16)
    o_ref[...] = x_ref[...] + jnp.dot(mix_scr[...], wo_ref[...], preferred_element_type=F32)
    if len(rest) == 4:
        _norm_rows(o_ref, rest[0], rest[2], row_chunk)


def _merge(ya, yb, yc, ym, proj, bg_all, wb_all, wo_all, l, x, next_g_all, *, tm):
    rows, d = x.shape
    tm = min(tm, rows)
    assert rows % tm == 0 and (U_GATES * BRANCH_W) % d == 0
    gate0 = U_GATES * BRANCH_W // d
    with_next = l + 1 < wo_all.shape[0]
    ysp = pl.BlockSpec((tm, BRANCH_W), lambda i: (i, 0))
    gl = lambda n: pl.BlockSpec((tm, d), lambda i, n=n: (i, gate0 + n))
    row_tile = pl.BlockSpec((tm, d), lambda i: (i, 0))
    est = (N_BRANCH * BRANCH_W * d * 2 + d * d * 2 + 2 * (4 * tm * BRANCH_W * 2 + 4 * tm * d * 4 + 2 * tm * d * 4)
           + tm * d * 2 + 4 * tm * d * 4 + 2 * tm * d * 2)
    in_specs = [ysp, ysp, ysp, ysp, gl(0), gl(1), gl(2), gl(3),
                _layer_spec(bg_all, l, pipeline_mode=pl.Buffered(1)),
                _layer_spec(wb_all, l, pipeline_mode=pl.Buffered(1)),
                _layer_spec(wo_all, l, pipeline_mode=pl.Buffered(1)),
                row_tile]
    args = [ya, yb, yc, ym, proj, proj, proj, proj, bg_all, wb_all, wo_all, x]
    out_specs, out_shape = [row_tile], [jax.ShapeDtypeStruct((rows, d), F32)]
    if with_next:
        in_specs.append(_layer_spec(next_g_all, l + 1))
        args.append(next_g_all)
        out_specs.append(row_tile)
        out_shape.append(jax.ShapeDtypeStruct((rows, d), BF16))
    outs = pl.pallas_call(
        functools.partial(_merge_kernel, col_chunk=BRANCH_W, row_chunk=min(tm, 32)),
        grid=(rows // tm,),
        in_specs=in_specs,
        out_specs=out_specs,
        out_shape=out_shape,
        scratch_shapes=[pltpu.VMEM((tm, d), BF16)],
        compiler_params=pltpu.CompilerParams(
            dimension_semantics=("arbitrary",), vmem_limit_bytes=_vmem_limit(est)),
        name="merge",
    )(*args)
    return (outs[0], outs[1]) if with_next else (outs[0], None)


def _gmlp_mask():
    i = jnp.arange(GMLP_CHUNK)
    return (i[None, :] // CHUNK) <= (i[:, None] // CHUNK)


def kernel(x_prompt, x_sample, mem_prompt, cache_sb_k, cache_sb_v, state_conv, cache_mem_k, cache_mem_v, norm_g, w_in, gmlp_ln_g, gmlp_ln_b, gmlp_ws, gmlp_bs, conv_w, conv_b, conv_ln_g, conv_ln_b, mem_norm_g, w_mem_kv, q_norm_g, k_norm_g, b_gate, w_branch, w_out):
    n_b, seq, d = x_prompt.shape
    dec_b, dec_t, _ = x_sample.shape
    depth = w_in.shape[0]
    n_mem = mem_prompt.shape[1]
    past_len = cache_sb_k.shape[2]
    assert seq % 256 == 0 and past_len % HEAD_DIM == 0 and dec_b * dec_t == GMLP_CHUNK and dec_t <= HALO
    assert GMLP_CHUNK % dec_t == 0 and d == N_BRANCH * BRANCH_W and dec_t % SUBLANES == 0
    assert seq % (SB_STREAMS_SHARED * HEAD_DIM) == 0 and dec_b % SB_STREAMS_OWN == 0

    w_in_bf = w_in.astype(BF16)
    w_mem_bf = w_mem_kv.astype(BF16)
    wb_bf = w_branch.astype(BF16)
    wo_bf = w_out.astype(BF16)
    row = lambda a: a.reshape(depth, 1, a.shape[-1])

    ws_m = jnp.where(_gmlp_mask()[None, None], gmlp_ws, 0.0)
    eye = jnp.eye(dec_b, dtype=F32)
    corner = ws_m[:, :, :dec_t, :dec_t]
    mix_w_s = jnp.einsum("bc,lgij->lgbicj", eye, corner).reshape(depth, N_HEADS, GMLP_CHUNK, GMLP_CHUNK)
    mix_b_s = jnp.tile(gmlp_bs[:, :, :dec_t], (1, 1, dec_b))
    lane_bcast = lambda a: jnp.broadcast_to(a[..., None], a.shape + (HEAD_DIM,))
    shared = dict(gmlp_ln_g=row(gmlp_ln_g), gmlp_ln_b=row(gmlp_ln_b), conv_w=conv_w, conv_b=row(conv_b),
                  conv_ln_g=row(conv_ln_g), conv_ln_b=row(conv_ln_b), q_norm_g=row(q_norm_g))
    params_p = dict(shared, mix_w=ws_m.astype(BF16), mix_b=lane_bcast(gmlp_bs))
    params_s = dict(shared, mix_w=mix_w_s.astype(BF16), mix_b=lane_bcast(mix_b_s))
    norm_g3, mem_norm_g3, k_norm_g3 = row(norm_g), row(mem_norm_g), row(k_norm_g)

    cache_k2 = cache_sb_k.reshape(depth * dec_b * past_len * N_HEADS, HEAD_DIM)
    cache_v2 = cache_sb_v.reshape(depth * dec_b * past_len * N_HEADS, HEAD_DIM)
    cache_mk = cache_mem_k.reshape(depth * dec_b, n_mem, BRANCH_W)
    cache_mv = cache_mem_v.reshape(depth * dec_b, n_mem, BRANCH_W)
    state32 = jnp.pad(state_conv, ((0, 0), (0, 0), (HALO - (CONV_W - 1), 0), (0, 0))).reshape(
        depth * dec_b, HALO, BRANCH_W)
    zero32 = jnp.zeros((n_b, HALO, BRANCH_W), F32)

    xp = x_prompt.reshape(n_b * seq, d)
    xs = x_sample.reshape(dec_b * dec_t, d)
    mem2 = mem_prompt.reshape(n_b * n_mem, d)
    tb_p = 256
    outs = {k: [] for k in ("cp", "mkp", "mvp", "cs", "gs")}
    fill = lambda rows: tuple(jnp.full((depth * rows * N_HEADS, HEAD_DIM), c, F32) for c in (0.0, 1.0))
    kv_p, kv_s = fill(n_b * seq), fill(dec_b * dec_t)
    hp = hs = None
    for l in range(depth):
        mk, mv = _memkv(mem2, mem_norm_g3, w_mem_bf, k_norm_g3, l)
        mk = mk.reshape(n_b, n_mem, BRANCH_W)
        mv = mv.reshape(n_b, n_mem, BRANCH_W)
        proj, kf, vf, kb, vb = _inproj(xp if hp is None else hp, norm_g3, w_in_bf, l, kv_p, tm=1024, tn=1024)
        kv_p = (kf, vf)
        yb = _sb_attn(proj, U_BQ, proj, U_BG, kb, 0, vb, 0, kb, vb, 0, n_groups=n_b,
                      n_steps=seq // (SB_STREAMS_SHARED * HEAD_DIM), n_streams=SB_STREAMS_SHARED, q_rows=HEAD_DIM,
                      past_rows=seq, shared_past=True)
        ya, yc, ym, _, nconv = _branches(proj, zero32, mk, mv, 0, params_p, l,
                                         nb=1, tb=tb_p, n_seg_tiles=n_b, n_row_tiles=seq // tb_p)
        xp, hp = _merge(ya, yb, yc, ym, proj, b_gate, wb_bf, wo_bf, l, xp, norm_g3, tm=256)
        outs["cp"].append(nconv[:, HALO - (CONV_W - 1):])
        outs["mkp"].append(mk.reshape(n_b, n_mem, N_HEADS, HEAD_DIM))
        outs["mvp"].append(mv.reshape(n_b, n_mem, N_HEADS, HEAD_DIM))

        proj_s, kf_s, vf_s, _, _ = _inproj(xs if hs is None else hs, norm_g3, w_in_bf, l, kv_s, tm=GMLP_CHUNK,
                                           tn=1024)
        kv_s = (kf_s, vf_s)
        yb_s = _sb_attn(proj_s, U_BQ, proj_s, U_BG, proj_s, U_BK, proj_s, U_BV, cache_k2, cache_v2, l * dec_b,
                        n_groups=1, n_steps=dec_b // SB_STREAMS_OWN, n_streams=SB_STREAMS_OWN, q_rows=dec_t,
                        past_rows=past_len, shared_past=False)
        ya_s, yc_s, ym_s, vn_s, nconv_s = _branches(proj_s, state32, cache_mk, cache_mv, l, params_s, l,
                                                    nb=dec_b, tb=dec_t, n_seg_tiles=1, n_row_tiles=1)
        xs, hs = _merge(ya_s, yb_s, yc_s, ym_s, proj_s, b_gate, wb_bf, wo_bf, l, xs, norm_g3, tm=GMLP_CHUNK)
        outs["cs"].append(nconv_s[:, HALO - (CONV_W - 1):])
        outs["gs"].append(vn_s.reshape(dec_b, dec_t, BRANCH_W))

    st = lambda k: jnp.stack(outs[k])
    heads_p = lambda a: a.reshape(depth, n_b, seq, N_HEADS, HEAD_DIM)
    heads_s = lambda a: a.reshape(depth, dec_b, dec_t, N_HEADS, HEAD_DIM)
    return (xp.reshape(n_b, seq, d), xs.reshape(dec_b, dec_t, d), heads_p(kv_p[0]), heads_p(kv_p[1]), st("cp"),
            st("mkp"), st("mvp"), heads_s(kv_s[0]), heads_s(kv_s[1]), st("cs"), st("gs"))
```
